```python
import math
import numpy as np
import jax
import jax.numpy as jnp
from jax import lax

D_MODEL = 2048
BATCH = 2
SEQ = 8192
DEPTH = 4

PLE_DIM = 256
EPS = 1e-6
QBLK = 128
NEG_INF = -1e30
BIG = 1e6

NSA_HEADS = 8
NSA_KV = 2
NSA_GROUP = NSA_HEADS // NSA_KV
NSA_HD = 128
CMP_STRIDE = 16
CMP_LEN = 2 * CMP_STRIDE
SEL_LEN = 64
SEL_TOPK = 16
WIN = 512
N_BRANCH = 3

DIFF_HEADS = 8
DIFF_QK = 64
DIFF_V = 2 * DIFF_QK

A_Q = NSA_HEADS * NSA_HD
A_KV = NSA_KV * NSA_HD
A_GATE = NSA_HEADS * N_BRANCH
B_QK = DIFF_HEADS * 2 * DIFF_QK
B_V = DIFF_HEADS * DIFF_V
ATTN_IN = A_Q + 6 * A_KV + A_GATE + 2 * B_QK + B_V
MIX_OUT = A_Q + B_V

LRU_W = D_MODEL
LRU_BLOCKS = 8
LRU_BW = LRU_W // LRU_BLOCKS
CONV_W = 4
LRU_C = 8.0

D_FF = -(-8 * D_MODEL // (3 * 256)) * 256

N_ATTN = (DEPTH + 1) // 2
N_REC = DEPTH // 2

kernel_name = "hybrid_nsa_diff_rglru_trunk"


def rmsnorm(x, g):
    x32 = x.astype(jnp.float32)
    y = x32 * lax.rsqrt(jnp.mean(x32 * x32, axis=-1, keepdims=True) + EPS)
    return (y * g.astype(jnp.float32)).astype(x.dtype)


def masked_softmax(s, valid):
    s = jnp.where(valid, s.astype(jnp.float32), NEG_INF)
    return jax.nn.softmax(s, axis=-1) * valid


def compress(kv, pe, w1, w2):
    b, s, h, d = kv.shape
    ch = kv.reshape(b, s // CMP_STRIDE, CMP_STRIDE, h, d)
    lo = ch[:, :-1] + pe[:CMP_STRIDE, None, :]
    hi = ch[:, 1:] + pe[CMP_STRIDE:, None, :]
    z = (jnp.einsum('bnlhd,lde->bnhe', lo, w1[:CMP_STRIDE])
         + jnp.einsum('bnlhd,lde->bnhe', hi, w1[CMP_STRIDE:]))
    return jnp.einsum('bnhe,ef->bnhf', jax.nn.gelu(z), w2)


def cmp_to_sel_matrix(n_cmp, n_sel):
    c0 = np.arange(n_cmp)[:, None] * CMP_STRIDE
    s0 = np.arange(n_sel)[None, :] * SEL_LEN
    ov = np.minimum(c0 + CMP_LEN, s0 + SEL_LEN) - np.maximum(c0, s0)
    return jnp.asarray(np.clip(ov, 0, None) / CMP_LEN, dtype=jnp.float32)


def nsa_attention(q, kc, vc, ks, vs, kw, vw, gates):
    b, s = q.shape[:2]
    n_cmp = kc.shape[1]
    n_sel = s // SEL_LEN
    topk = min(SEL_TOPK, n_sel)
    scale = NSA_HD ** -0.5
    sel_map = cmp_to_sel_matrix(n_cmp, n_sel)
    cmp_end = jnp.arange(n_cmp) * CMP_STRIDE + CMP_LEN - 1
    blk = jnp.arange(n_sel)
    ks_b = ks.reshape(b, n_sel, SEL_LEN, NSA_KV, NSA_HD).transpose(0, 3, 1, 2, 4)
    vs_b = vs.reshape(b, n_sel, SEL_LEN, NSA_KV, NSA_HD).transpose(0, 3, 1, 2, 4)
    kw_p = jnp.pad(kw, ((0, 0), (WIN, 0), (0, 0), (0, 0)))
    vw_p = jnp.pad(vw, ((0, 0), (WIN, 0), (0, 0), (0, 0)))
    bi = jnp.arange(b)[:, None, None, None]
    hi = jnp.arange(NSA_KV)[None, :, None, None]

    def block(qb):
        start = qb * QBLK
        t = start + jnp.arange(QBLK)
        qq = lax.dynamic_slice_in_dim(q, start, QBLK, 1) * scale
        gg = jax.nn.sigmoid(lax.dynamic_slice_in_dim(gates, start, QBLK, 1).astype(jnp.float32))
        s_c = jnp.einsum('bqhgd,bnhd->bhgqn', qq, kc)
        p_c = masked_softmax(s_c, cmp_end[None, :] <= t[:, None])
        o_c = jnp.einsum('bhgqn,bnhd->bqhgd', p_c, vc)
        imp = jnp.einsum('bhgqn,nj->bhqj', p_c, sel_map)
        cur = (t // SEL_LEN)[:, None]
        forced = (blk == 0) | (blk == cur) | (blk == cur - 1)
        imp = jnp.where(blk > cur, -BIG, jnp.where(forced, BIG, imp))
        _, idx = lax.top_k(imp, topk)
        kg = ks_b[bi, hi, idx]
        vg = vs_b[bi, hi, idx]
        pos = idx[..., None] * SEL_LEN + jnp.arange(SEL_LEN)
        valid_s = (pos <= t[:, None, None]).reshape(b, NSA_KV, 1, QBLK, topk * SEL_LEN)
        s_s = jnp.einsum('bqhgd,bhqkld->bhgqkl', qq, kg).reshape(b, NSA_KV, NSA_GROUP, QBLK, topk * SEL_LEN)
        p_s = masked_softmax(s_s, valid_s).reshape(b, NSA_KV, NSA_GROUP, QBLK, topk, SEL_LEN)
        o_s = jnp.einsum('bhgqkl,bhqkld->bqhgd', p_s, vg)
        kwb = lax.dynamic_slice_in_dim(kw_p, start, QBLK + WIN, 1)
        vwb = lax.dynamic_slice_in_dim(vw_p, start, QBLK + WIN, 1)
        kpos = start - WIN + jnp.arange(QBLK + WIN)
        dlt = t[:, None] - kpos[None, :]
        valid_w = (dlt >= 0) & (dlt < WIN) & (kpos[None, :] >= 0)
        s_w = jnp.einsum('bqhgd,bkhd->bhgqk', qq, kwb)
        p_w = masked_softmax(s_w, valid_w)
        o_w = jnp.einsum('bhgqk,bkhd->bqhgd', p_w, vwb)
        o = gg[..., 0:1] * o_c + gg[..., 1:2] * o_s + gg[..., 2:3] * o_w
        return o.reshape(b, QBLK, A_Q).astype(q.dtype)

    out = lax.map(block, jnp.arange(s // QBLK))
    return out.transpose(1, 0, 2, 3).reshape(b, s, A_Q)


def diff_attention(q, k, v, lam):
    b, s = q.shape[:2]
    scale = DIFF_QK ** -0.5
    kpos = jnp.arange(s)

    def block(qb):
        start = qb * QBLK
        t = start + jnp.arange(QBLK)
        qq = lax.dynamic_slice_in_dim(q, start, QBLK, 1) * scale
        sc = jnp.einsum('bqhcd,bkhcd->cbhqk', qq, k)
        pr = masked_softmax(sc, kpos[None, :] <= t[:, None])
        a = pr[0] - lam * pr[1]
        return jnp.einsum('bhqk,bkhd->bqhd', a, v).astype(v.dtype)

    out = lax.map(block, jnp.arange(s // QBLK))
    return out.transpose(1, 0, 2, 3, 4).reshape(b, s, DIFF_HEADS, DIFF_V)


def attn_mixer(xn, layer_idx, w_in, pe_k, w1_k, w2_k, pe_v, w1_v, w2_v,
               lq1, lk1, lq2, lk2, subln, w_out):
    b, s, _ = xn.shape
    proj = xn @ w_in
    o1 = A_Q
    o2 = o1 + 6 * A_KV
    o3 = o2 + A_GATE
    o4 = o3 + B_QK
    o5 = o4 + B_QK
    q_a, kv_a, g_a, q_b, k_b, v_b = jnp.split(proj, [o1, o2, o3, o4, o5], axis=-1)
    kv_a = kv_a.reshape(b, s, 6, NSA_KV, NSA_HD)
    kc = compress(kv_a[:, :, 0], pe_k, w1_k, w2_k)
    vc = compress(kv_a[:, :, 1], pe_v, w1_v, w2_v)
    o_a = nsa_attention(q_a.reshape(b, s, NSA_KV, NSA_GROUP, NSA_HD), kc, vc,
                        kv_a[:, :, 2], kv_a[:, :, 3], kv_a[:, :, 4], kv_a[:, :, 5],
                        g_a.reshape(b, s, NSA_KV, NSA_GROUP, N_BRANCH))
    lam_init = 0.8 - 0.6 * math.exp(-0.3 * layer_idx)
    f32 = jnp.float32
    lam = (jnp.exp(jnp.sum(lq1.astype(f32) * lk1.astype(f32)))
           - jnp.exp(jnp.sum(lq2.astype(f32) * lk2.astype(f32))) + lam_init)
    o_b = diff_attention(q_b.reshape(b, s, DIFF_HEADS, 2, DIFF_QK),
                         k_b.reshape(b, s, DIFF_HEADS, 2, DIFF_QK),
                         v_b.reshape(b, s, DIFF_HEADS, DIFF_V), lam)
    o_b = rmsnorm(o_b, subln) * (1.0 - lam_init)
    o = jnp.concatenate([o_a, o_b.reshape(b, s, B_V)], axis=-1)
    return o @ w_out


def rglru_mixer(xn, w_in, conv_w, conv_b, wa, ba, wx, bx, lam_p, w_out):
    b, s, _ = xn.shape
    xb, yb = jnp.split(xn @ w_in, 2, axis=-1)
    y = jax.nn.gelu(yb)
    xc = lax.conv_general_dilated(xb, conv_w[:, None, :], window_strides=(1,),
                                  padding=[(CONV_W - 1, 0)],
                                  dimension_numbers=('NWC', 'WIO', 'NWC'),
                                  feature_group_count=LRU_W) + conv_b
    xg = xc.reshape(b, s, LRU_BLOCKS, LRU_BW)
    r = jax.nn.sigmoid(jnp.einsum('bsnd,nde->bsne', xg, wa).reshape(b, s, LRU_W) + ba)
    i = jax.nn.sigmoid(jnp.einsum('bsnd,nde->bsne', xg, wx).reshape(b, s, LRU_W) + bx)
    log_a = -LRU_C * r.astype(jnp.float32) * jax.nn.softplus(-lam_p.astype(jnp.float32))
    a = jnp.exp(log_a)
    u = jnp.sqrt(-jnp.expm1(2.0 * log_a)) * (i * xc).astype(jnp.float32)

    def comb(e1, e2):
        a1, b1 = e1
        a2, b2 = e2
        return a1 * a2, a2 * b1 + b2

    _, h = lax.associative_scan(comb, (a, u), axis=1)
    return (h.astype(xn.dtype) * y) @ w_out


def swiglu(xn, wg, wu, wd):
    return (jax.nn.silu(xn @ wg) * (xn @ wu)) @ wd


def _normal(key, shape, scale):
    return jax.random.normal(key, shape, jnp.float32) * scale


def setup_inputs(seed: int = 0) -> dict:
    key = jax.random.key(seed)
    ks = jax.random.split(key, 33)
    na, nr = N_ATTN, N_REC
    out_scale = (2 * DEPTH) ** -0.5
    a0 = jax.random.uniform(ks[26], (nr, LRU_W), jnp.float32, 0.9, 0.999)
    return {
        "x": _normal(ks[0], (BATCH, SEQ, D_MODEL), 1.0),
        "p": _normal(ks[1], (DEPTH, BATCH, SEQ, PLE_DIM), 1.0),
        "g_mix": 1.0 + _normal(ks[2], (DEPTH, D_MODEL), 0.02),
        "g_ffn": 1.0 + _normal(ks[3], (DEPTH, D_MODEL), 0.02),
        "g_ple": 1.0 + _normal(ks[4], (DEPTH, D_MODEL), 0.02),
        "g_final": 1.0 + _normal(ks[5], (D_MODEL,), 0.02),
        "w_in_attn": _normal(ks[6], (na, D_MODEL, ATTN_IN), D_MODEL ** -0.5),
        "cmp_pe_k": _normal(ks[7], (na, CMP_LEN, NSA_HD), 0.5),
        "cmp_w1_k": _normal(ks[8], (na, CMP_LEN, NSA_HD, NSA_HD), (CMP_LEN * NSA_HD) ** -0.5),
        "cmp_w2_k": _normal(ks[9], (na, NSA_HD, NSA_HD), NSA_HD ** -0.5),
        "cmp_pe_v": _normal(ks[10], (na, CMP_LEN, NSA_HD), 0.5),
        "cmp_w1_v": _normal(ks[11], (na, CMP_LEN, NSA_HD, NSA_HD), (CMP_LEN * NSA_HD) ** -0.5),
        "cmp_w2_v": _normal(ks[12], (na, NSA_HD, NSA_HD), NSA_HD ** -0.5),
        "diff_lq1": _normal(ks[13], (na, DIFF_QK), 0.1),
        "diff_lk1": _normal(ks[14], (na, DIFF_QK), 0.1),
        "diff_lq2": _normal(ks[15], (na, DIFF_QK), 0.1),
        "diff_lk2": _normal(ks[16], (na, DIFF_QK), 0.1),
        "diff_subln": 1.0 + _normal(ks[17], (na, DIFF_HEADS, DIFF_V), 0.02),
        "w_out_attn": _normal(ks[18], (na, MIX_OUT, D_MODEL), MIX_OUT ** -0.5 * out_scale),
        "w_in_rec": _normal(ks[19], (nr, D_MODEL, 2 * LRU_W), D_MODEL ** -0.5),
        "conv_w": _normal(ks[20], (nr, CONV_W, LRU_W), CONV_W ** -0.5),
        "conv_b": _normal(ks[21], (nr, LRU_W), 0.01),
        "lru_wa": _normal(ks[22], (nr, LRU_BLOCKS, LRU_BW, LRU_BW), LRU_BW ** -0.5),
        "lru_ba": _normal(ks[23], (nr, LRU_W), 0.01),
        "lru_wx": _normal(ks[24], (nr, LRU_BLOCKS, LRU_BW, LRU_BW), LRU_BW ** -0.5),
        "lru_bx": _normal(ks[25], (nr, LRU_W), 0.01),
        "lru_lambda": jnp.log(a0) - jnp.log1p(-a0),
        "w_out_rec": _normal(ks[27], (nr, LRU_W, D_MODEL), LRU_W ** -0.5 * out_scale),
        "w_ffn_gate": _normal(ks[28], (DEPTH, D_MODEL, D_FF), D_MODEL ** -0.5),
        "w_ffn_up": _normal(ks[29], (DEPTH, D_MODEL, D_FF), D_MODEL ** -0.5),
        "w_ffn_down": _normal(ks[30], (DEPTH, D_FF, D_MODEL), D_FF ** -0.5 * out_scale),
        "w_ple_proj": _normal(ks[31], (DEPTH, PLE_DIM, D_MODEL), PLE_DIM ** -0.5),
        "w_ple_gate": _normal(ks[32], (DEPTH, D_MODEL, D_MODEL), D_MODEL ** -0.5),
    }


def reference(x, p, g_mix, g_ffn, g_ple, g_final, w_in_attn, cmp_pe_k, cmp_w1_k, cmp_w2_k,
              cmp_pe_v, cmp_w1_v, cmp_w2_v, diff_lq1, diff_lk1, diff_lq2, diff_lk2, diff_subln,
              w_out_attn, w_in_rec, conv_w, conv_b, lru_wa, lru_ba, lru_wx, lru_bx, lru_lambda,
              w_out_rec, w_ffn_gate, w_ffn_up, w_ffn_down, w_ple_proj, w_ple_gate):
    h = x
    for i in range(DEPTH):
        xn = rmsnorm(h, g_mix[i])
        j = i // 2
        if i % 2 == 0:
            mix = attn_mixer(xn, i, w_in_attn[j], cmp_pe_k[j], cmp_w1_k[j], cmp_w2_k[j],
                             cmp_pe_v[j], cmp_w1_v[j], cmp_w2_v[j], diff_lq1[j], diff_lk1[j],
                             diff_lq2[j], diff_lk2[j], diff_subln[j], w_out_attn[j])
        else:
            mix = rglru_mixer(xn, w_in_rec[j], conv_w[j], conv_b[j], lru_wa[j], lru_ba[j],
                              lru_wx[j], lru_bx[j], lru_lambda[j], w_out_rec[j])
        h = h + mix
        h = h + swiglu(rmsnorm(h, g_ffn[i]), w_ffn_gate[i], w_ffn_up[i], w_ffn_down[i])
        gate = jax.nn.sigmoid(rmsnorm(h, g_ple[i]) @ w_ple_gate[i])
        h = h + gate * (p[i] @ w_ple_proj[i])
    return rmsnorm(h, g_final)
```

```python
import functools
import math

import numpy as np
import jax
import jax.numpy as jnp
from jax import lax
from jax.experimental import pallas as pl
from jax.experimental.pallas import tpu as pltpu

F32 = jnp.float32
BF16 = jnp.bfloat16

EPS = 1e-6
NEG_INF = -1e30
BIG = 1e6

NSA_HEADS = 8
NSA_KV = 2
NSA_GROUP = NSA_HEADS // NSA_KV
HEAD_DIM = 128
CMP_STRIDE = 16
CMP_LEN = 2 * CMP_STRIDE
SEL_LEN = 64
SEL_TOPK = 16
WIN = 512
N_BRANCH = 3
DIFF_HEADS = 8
DIFF_QK = 64
CONV_W = 4
LRU_BLOCKS = 8
LRU_C = 8.0

LANES = 128
SUBLANES = 8
VMEM_LIMIT = 56 * 1024 * 1024

CB_QA = 0
CB_KCMP, CB_VCMP = 8, 10
CB_KSEL, CB_VSEL = 12, 14
CB_KWIN, CB_VWIN = 16, 18
CB_QB, CB_KB, CB_VB = 20, 28, 36
N_MAIN_COLS = 44 * LANES


def _cparams(*sem):
    return pltpu.CompilerParams(dimension_semantics=sem, vmem_limit_bytes=VMEM_LIMIT)


def _dot(a, b):
    return jnp.dot(a, b, preferred_element_type=F32)


def _dot_nt(a, b):
    return lax.dot_general(a, b, (((1,), (1,)), ((), ())), preferred_element_type=F32)


def _rms(x, g):
    return x * lax.rsqrt(jnp.mean(x * x, axis=-1, keepdims=True) + EPS) * g


def _norm_matmul_kernel(x_ref, g_ref, w_ref, s_ref, o_ref, xn_ref):
    @pl.when(pl.program_id(1) == 0)
    def _():
        xn_ref[...] = _rms(x_ref[...], g_ref[...]).astype(BF16)

    o_ref[...] = (_dot(xn_ref[...], w_ref[...]) * s_ref[...]).astype(o_ref.dtype)


def norm_matmul(x, g, w, col_scale, out_dtype, tm=512, tn=512):
    m, d = x.shape
    n = w.shape[1]
    tn = min(tn, n)
    return pl.pallas_call(
        _norm_matmul_kernel,
        grid=(m // tm, n // tn),
        in_specs=[pl.BlockSpec((tm, d), lambda i, j: (i, 0)),
                  pl.BlockSpec((1, d), lambda i, j: (0, 0)),
                  pl.BlockSpec((d, tn), lambda i, j: (0, j)),
                  pl.BlockSpec((1, tn), lambda i, j: (0, j))],
        out_specs=pl.BlockSpec((tm, tn), lambda i, j: (i, j)),
        out_shape=jax.ShapeDtypeStruct((m, n), out_dtype),
        scratch_shapes=[pltpu.VMEM((tm, d), BF16)],
        compiler_params=_cparams("parallel", "arbitrary"),
        name="norm_matmul",
    )(x, g, w, col_scale)


def _matmul_res_kernel(*refs):
    n_ops = (len(refs) - 2) // 2
    res_ref, o_ref = refs[2 * n_ops], refs[2 * n_ops + 1]
    acc = res_ref[...]
    for a_ref, w_ref in zip(refs[:n_ops], refs[n_ops:2 * n_ops]):
        acc = acc + _dot(a_ref[...], w_ref[...])
    o_ref[...] = acc


def matmul_residual(a_list, w_list, res, tm=512, tn=512):
    m, n = res.shape
    in_specs = [pl.BlockSpec((tm, a.shape[1]), lambda i, j: (i, 0)) for a in a_list]
    in_specs += [pl.BlockSpec((w.shape[0], tn), lambda i, j: (0, j)) for w in w_list]
    in_specs += [pl.BlockSpec((tm, tn), lambda i, j: (i, j))]
    return pl.pallas_call(
        _matmul_res_kernel,
        grid=(m // tm, n // tn),
        in_specs=in_specs,
        out_specs=pl.BlockSpec((tm, tn), lambda i, j: (i, j)),
        out_shape=jax.ShapeDtypeStruct((m, n), F32),
        compiler_params=_cparams("parallel", "arbitrary"),
        name="matmul_residual",
    )(*a_list, *w_list, res)


def _ffn_kernel(x_ref, g_ref, wg_ref, wu_ref, wd_ref, o_ref, xn_ref, acc_ref):
    f = pl.program_id(1)

    @pl.when(f == 0)
    def _():
        xn_ref[...] = _rms(x_ref[...], g_ref[...]).astype(BF16)
        acc_ref[...] = jnp.zeros_like(acc_ref)

    xn = xn_ref[...]
    act = jax.nn.silu(_dot(xn, wg_ref[...])) * _dot(xn, wu_ref[...])
    acc_ref[...] += _dot(act.astype(BF16), wd_ref[...])

    @pl.when(f == pl.num_programs(1) - 1)
    def _():
        o_ref[...] = x_ref[...] + acc_ref[...]


def ffn(x, g, wg, wu, wd, tm=512, tf=512):
    m, d = x.shape
    dff = wg.shape[1]
    return pl.pallas_call(
        _ffn_kernel,
        grid=(m // tm, dff // tf),
        in_specs=[pl.BlockSpec((tm, d), lambda i, f: (i, 0)),
                  pl.BlockSpec((1, d), lambda i, f: (0, 0)),
                  pl.BlockSpec((d, tf), lambda i, f: (0, f)),
                  pl.BlockSpec((d, tf), lambda i, f: (0, f)),
                  pl.BlockSpec((tf, d), lambda i, f: (f, 0))],
        out_specs=pl.BlockSpec((tm, d), lambda i, f: (i, 0)),
        out_shape=jax.ShapeDtypeStruct((m, d), F32),
        scratch_shapes=[pltpu.VMEM((tm, d), BF16), pltpu.VMEM((tm, d), F32)],
        compiler_params=_cparams("parallel", "arbitrary"),
        name="ffn",
    )(x, g, wg, wu, wd)


def _ple_kernel(x_ref, xc_ref, g_ref, p_ref, wg_ref, wp_ref, o_ref, xn_ref):
    @pl.when(pl.program_id(1) == 0)
    def _():
        xn_ref[...] = _rms(x_ref[...], g_ref[...]).astype(BF16)

    gate = jax.nn.sigmoid(_dot(xn_ref[...], wg_ref[...]))
    o_ref[...] = xc_ref[...] + gate * _dot(p_ref[...].astype(BF16), wp_ref[...])


def ple(x, g, p, wgate, wproj, tm=512, tn=512):
    m, d = x.shape
    pd = p.shape[1]
    return pl.pallas_call(
        _ple_kernel,
        grid=(m // tm, d // tn),
        in_specs=[pl.BlockSpec((tm, d), lambda i, j: (i, 0)),
                  pl.BlockSpec((tm, tn), lambda i, j: (i, j)),
                  pl.BlockSpec((1, d), lambda i, j: (0, 0)),
                  pl.BlockSpec((tm, pd), lambda i, j: (i, 0)),
                  pl.BlockSpec((d, tn), lambda i, j: (0, j)),
                  pl.BlockSpec((pd, tn), lambda i, j: (0, j))],
        out_specs=pl.BlockSpec((tm, tn), lambda i, j: (i, j)),
        out_shape=jax.ShapeDtypeStruct((m, d), F32),
        scratch_shapes=[pltpu.VMEM((tm, d), BF16)],
        compiler_params=_cparams("parallel", "arbitrary"),
        name="ple",
    )(x, x, g, p, wgate, wproj)


def _final_norm_kernel(x_ref, g_ref, o_ref):
    o_ref[...] = _rms(x_ref[...], g_ref[...])


def final_norm(x, g, tm=512):
    m, d = x.shape
    return pl.pallas_call(
        _final_norm_kernel,
        grid=(m // tm,),
        in_specs=[pl.BlockSpec((tm, d), lambda i: (i, 0)),
                  pl.BlockSpec((1, d), lambda i: (0, 0))],
        out_specs=pl.BlockSpec((tm, d), lambda i: (i, 0)),
        out_shape=jax.ShapeDtypeStruct((m, d), F32),
        compiler_params=_cparams("parallel"),
        name="final_norm",
    )(x, g)


def _compress_kernel(x_ref, pelo_ref, pehi_ref, w1lo_ref, w1hi_ref, w2_ref, o_ref):
    x = x_ref[...].astype(F32)
    nb = x.shape[0]
    lo = _dot((x + pelo_ref[...]).astype(BF16), w1lo_ref[...])
    hi = _dot((x + pehi_ref[...]).astype(BF16), w1hi_ref[...])
    z = lo + pltpu.roll(hi, shift=nb - 1, axis=0)
    y = _dot(jax.nn.gelu(z).astype(BF16), w2_ref[...])
    row = lax.broadcasted_iota(jnp.int32, y.shape, 0)
    o_ref[...] = jnp.where(row < nb - 1, y, 0.0).astype(o_ref.dtype)


def compress(x4, pelo, pehi, w1lo, w1hi, w2):
    b, _, nb, kd = x4.shape
    sel = lambda bi, c: (c // NSA_KV, 0, 0)
    return pl.pallas_call(
        _compress_kernel,
        grid=(b, 2 * NSA_KV),
        in_specs=[pl.BlockSpec((None, None, nb, kd), lambda bi, c: (bi, c, 0, 0)),
                  pl.BlockSpec((None, 1, kd), sel),
                  pl.BlockSpec((None, 1, kd), sel),
                  pl.BlockSpec((None, kd, HEAD_DIM), sel),
                  pl.BlockSpec((None, kd, HEAD_DIM), sel),
                  pl.BlockSpec((None, HEAD_DIM, HEAD_DIM), sel)],
        out_specs=pl.BlockSpec((None, None, nb, HEAD_DIM), lambda bi, c: (bi, c, 0, 0)),
        out_shape=jax.ShapeDtypeStruct((b, 2 * NSA_KV, nb, HEAD_DIM), BF16),
        compiler_params=_cparams("parallel", "parallel"),
        name="nsa_compress",
    )(x4, pelo, pehi, w1lo, w1hi, w2)


def _stack_heads(q):
    return jnp.concatenate([q[:, g * HEAD_DIM:(g + 1) * HEAD_DIM] for g in range(NSA_GROUP)], axis=0)


def _nsa_cmp_kernel(q_ref, kc_ref, vc_ref, gate_ref, selmap_ref, oc_ref, sel_ref, *, n_cmp, n_sel, topk):
    tq = q_ref.shape[0]
    h = pl.program_id(1)
    s0 = pl.program_id(2) * tq
    q4 = _stack_heads(q_ref[...])
    s = _dot_nt(q4, kc_ref[...])
    t1 = s0 + lax.broadcasted_iota(jnp.int32, (tq, 1), 0)
    t = jnp.concatenate([t1] * NSA_GROUP, axis=0)
    n = lax.broadcasted_iota(jnp.int32, (1, s.shape[1]), 1)
    valid = (n * CMP_STRIDE + (CMP_LEN - 1) <= t) & (n < n_cmp)
    sm = jnp.where(valid, s, NEG_INF)
    e = jnp.where(valid, jnp.exp(sm - jnp.max(sm, axis=-1, keepdims=True)), 0.0)
    l = jnp.sum(e, axis=-1, keepdims=True)
    p = e / jnp.where(l > 0.0, l, 1.0)
    o = _dot(p.astype(BF16), vc_ref[...])

    gates = jax.nn.sigmoid(gate_ref[...])
    lane = lax.broadcasted_iota(jnp.int32, (1, LANES), 1)
    for g in range(NSA_GROUP):
        col = (h * NSA_GROUP + g) * N_BRANCH
        gsel = jnp.sum(jnp.where(lane == col, gates, 0.0), axis=-1, keepdims=True)
        oc_ref[:, g * HEAD_DIM:(g + 1) * HEAD_DIM] = gsel * o[g * tq:(g + 1) * tq]

    ps = p[0:tq]
    for g in range(1, NSA_GROUP):
        ps = ps + p[g * tq:(g + 1) * tq]
    ps_hi = ps.astype(BF16)
    ps_lo = (ps - ps_hi.astype(F32)).astype(BF16)
    imp = _dot(ps_hi, selmap_ref[...]) + _dot(ps_lo, selmap_ref[...])
    cur = t1 // SEL_LEN
    forced = (lane == 0) | (lane == cur) | (lane == cur - 1)
    imp = jnp.where(lane > cur, -BIG, jnp.where(forced, BIG, imp))
    imp = jnp.where(lane < n_sel, imp, -jnp.inf)
    sel = jnp.zeros(imp.shape, F32)
    for _ in range(topk):
        mx = jnp.max(imp, axis=-1, keepdims=True)
        idx = jnp.min(jnp.where(imp == mx, lane, LANES), axis=-1, keepdims=True)
        pick = lane == idx
        sel = jnp.where(pick, 1.0, sel)
        imp = jnp.where(pick, -jnp.inf, imp)
    sel_ref[...] = sel.astype(sel_ref.dtype)


def nsa_cmp_select(proj, kvc, gates, selmap, batch, seq, tq=256):
    nb = kvc.shape[2]
    nq = seq // tq
    n_sel = seq // SEL_LEN
    kern = functools.partial(_nsa_cmp_kernel, n_cmp=nb - 1, n_sel=n_sel, topk=min(SEL_TOPK, n_sel))
    gw = NSA_GROUP * HEAD_DIM
    return pl.pallas_call(
        kern,
        grid=(batch, NSA_KV, nq),
        in_specs=[pl.BlockSpec((tq, gw), lambda b, h, i: (b * nq + i, h)),
                  pl.BlockSpec((None, None, nb, HEAD_DIM), lambda b, h, i: (b, h, 0, 0)),
                  pl.BlockSpec((None, None, nb, HEAD_DIM), lambda b, h, i: (b, NSA_KV + h, 0, 0)),
                  pl.BlockSpec((tq, LANES), lambda b, h, i: (b * nq + i, 0)),
                  pl.BlockSpec((nb, LANES), lambda b, h, i: (0, 0))],
        out_specs=[pl.BlockSpec((tq, gw), lambda b, h, i: (b * nq + i, h)),
                   pl.BlockSpec((None, None, tq, LANES), lambda b, h, i: (b, h, i, 0))],
        out_shape=[jax.ShapeDtypeStruct((batch * seq, NSA_HEADS * HEAD_DIM), F32),
                   jax.ShapeDtypeStruct((batch, NSA_KV, seq, LANES), BF16)],
        compiler_params=_cparams("parallel", "parallel", "arbitrary"),
        name="nsa_cmp_select",
    )(proj, kvc, kvc, gates, selmap)


def _nsa_sel_win_kernel(q_ref, ks_ref, vs_ref, kw_ref, vw_ref, sel_ref, expand_ref, gate_ref, oc_ref,
                        o_ref, m_ref, l_ref, acc_ref, *, tk):
    tq = q_ref.shape[0]
    rows = NSA_GROUP * tq
    h = pl.program_id(1)
    s0 = pl.program_id(2) * tq
    q4 = _stack_heads(q_ref[...])
    t1 = s0 + lax.broadcasted_iota(jnp.int32, (tq, 1), 0)
    sel = sel_ref[...]

    m_ref[...] = jnp.full(m_ref.shape, NEG_INF, F32)
    l_ref[...] = jnp.zeros(l_ref.shape, F32)
    acc_ref[...] = jnp.zeros(acc_ref.shape, F32)

    def step(kb, causal):
        k0 = pl.multiple_of(kb * tk, tk)
        s = _dot_nt(q4, ks_ref[pl.ds(k0, tk), :])
        chosen = _dot_nt(sel, expand_ref[pl.ds(k0, tk), :]) > 0.5
        if causal:
            kpos = k0 + lax.broadcasted_iota(jnp.int32, (1, tk), 1)
            chosen = chosen & (kpos <= t1)
        s = jnp.concatenate([jnp.where(chosen, s[g * tq:(g + 1) * tq], NEG_INF) for g in range(NSA_GROUP)], axis=0)
        m_prev = m_ref[...]
        m_new = jnp.maximum(m_prev, jnp.max(s, axis=-1, keepdims=True))
        p = jnp.exp(s - m_new)
        alpha = jnp.exp(m_prev - m_new)
        l_ref[...] = alpha * l_ref[...] + jnp.sum(p, axis=-1, keepdims=True)
        acc_ref[...] = alpha * acc_ref[...] + _dot(p.astype(BF16), vs_ref[pl.ds(k0, tk), :])
        m_ref[...] = m_new

    n_full = (s0 + 1) // tk
    n_all = (s0 + tq - 1) // tk + 1

    def full_body(kb, c):
        step(kb, False)
        return c

    def diag_body(kb, c):
        step(kb, True)
        return c

    lax.fori_loop(0, n_full, full_body, 0)
    lax.fori_loop(n_full, n_all, diag_body, 0)
    o_sel = acc_ref[...] / l_ref[...]

    nw = WIN + tq
    w0 = pl.multiple_of(jnp.maximum(s0 - WIN, 0), tq)
    sw = _dot_nt(q4, kw_ref[pl.ds(w0, nw), :])
    dlt = t1 - (w0 + lax.broadcasted_iota(jnp.int32, (1, nw), 1))
    inwin = (dlt >= 0) & (dlt < WIN)
    sw = jnp.concatenate([jnp.where(inwin, sw[g * tq:(g + 1) * tq], NEG_INF) for g in range(NSA_GROUP)], axis=0)
    pw = jnp.exp(sw - jnp.max(sw, axis=-1, keepdims=True))
    o_win = _dot(pw.astype(BF16), vw_ref[pl.ds(w0, nw), :]) / jnp.sum(pw, axis=-1, keepdims=True)

    gates = jax.nn.sigmoid(gate_ref[...])
    lane = lax.broadcasted_iota(jnp.int32, (1, LANES), 1)
    for g in range(NSA_GROUP):
        col = (h * NSA_GROUP + g) * N_BRANCH
        g_sel = jnp.sum(jnp.where(lane == col + 1, gates, 0.0), axis=-1, keepdims=True)
        g_win = jnp.sum(jnp.where(lane == col + 2, gates, 0.0), axis=-1, keepdims=True)
        rs = slice(g * tq, (g + 1) * tq)
        cs = slice(g * HEAD_DIM, (g + 1) * HEAD_DIM)
        o_ref[:, cs] = (oc_ref[:, cs] + g_sel * o_sel[rs] + g_win * o_win[rs]).astype(o_ref.dtype)


def nsa_sel_win(proj, selmask, expand, gates, oc, batch, seq, tq=128, tk=512):
    nq = seq // tq
    tk = min(tk, seq)
    gw = NSA_GROUP * HEAD_DIM
    rows = NSA_GROUP * tq
    kv_spec = lambda cb: pl.BlockSpec((seq, HEAD_DIM), lambda b, h, i: (b, cb + h))
    return pl.pallas_call(
        functools.partial(_nsa_sel_win_kernel, tk=tk),
        grid=(batch, NSA_KV, nq),
        in_specs=[pl.BlockSpec((tq, gw), lambda b, h, i: (b * nq + i, h)),
                  kv_spec(CB_KSEL), kv_spec(CB_VSEL), kv_spec(CB_KWIN), kv_spec(CB_VWIN),
                  pl.BlockSpec((None, None, tq, LANES), lambda b, h, i: (b, h, i, 0)),
                  pl.BlockSpec((seq, LANES), lambda b, h, i: (0, 0)),
                  pl.BlockSpec((tq, LANES), lambda b, h, i: (b * nq + i, 0)),
                  pl.BlockSpec((tq, gw), lambda b, h, i: (b * nq + i, h))],
        out_specs=pl.BlockSpec((tq, gw), lambda b, h, i: (b * nq + i, h)),
        out_shape=jax.ShapeDtypeStruct((batch * seq, NSA_HEADS * HEAD_DIM), BF16),
        scratch_shapes=[pltpu.VMEM((rows, 1), F32), pltpu.VMEM((rows, 1), F32),
                        pltpu.VMEM((rows, HEAD_DIM), F32)],
        compiler_params=_cparams("parallel", "parallel", "arbitrary"),
        name="nsa_sel_win",
    )(proj, proj, proj, proj, proj, selmask, expand, gates, oc)


def _diff_attn_kernel(q_ref, k_ref, v_ref, lamp_ref, subln_ref, o_ref, m_ref, l_ref, acc_ref, *, tk, lam_init):
    tq = q_ref.shape[0]
    s0 = pl.program_id(2) * tq
    q = q_ref[...]
    lane = lax.broadcasted_iota(jnp.int32, (1, LANES), 1)
    zero = jnp.zeros_like(q)
    q2 = jnp.concatenate([jnp.where(lane < DIFF_QK, q, zero), jnp.where(lane >= DIFF_QK, q, zero)], axis=0)
    t = s0 + lax.broadcasted_iota(jnp.int32, (tq, 1), 0)
    t2 = jnp.concatenate([t, t], axis=0)

    m_ref[...] = jnp.full(m_ref.shape, NEG_INF, F32)
    l_ref[...] = jnp.zeros(l_ref.shape, F32)
    acc_ref[...] = jnp.zeros(acc_ref.shape, F32)

    def step(kb, causal):
        k0 = pl.multiple_of(kb * tk, tk)
        s = _dot_nt(q2, k_ref[pl.ds(k0, tk), :])
        if causal:
            kpos = k0 + lax.broadcasted_iota(jnp.int32, (1, tk), 1)
            s = jnp.where(kpos <= t2, s, NEG_INF)
        m_prev = m_ref[...]
        m_new = jnp.maximum(m_prev, jnp.max(s, axis=-1, keepdims=True))
        p = jnp.exp(s - m_new)
        alpha = jnp.exp(m_prev - m_new)
        l_ref[...] = alpha * l_ref[...] + jnp.sum(p, axis=-1, keepdims=True)
        acc_ref[...] = alpha * acc_ref[...] + _dot(p.astype(BF16), v_ref[pl.ds(k0, tk), :])
        m_ref[...] = m_new

    n_full = (s0 + 1) // tk
    n_all = (s0 + tq - 1) // tk + 1

    def full_body(kb, c):
        step(kb, False)
        return c

    def diag_body(kb, c):
        step(kb, True)
        return c

    lax.fori_loop(0, n_full, full_body, 0)
    lax.fori_loop(n_full, n_all, diag_body, 0)

    lp = lamp_ref[...]
    lam = (jnp.exp(jnp.sum(lp[0:1] * lp[1:2], axis=-1, keepdims=True))
           - jnp.exp(jnp.sum(lp[2:3] * lp[3:4], axis=-1, keepdims=True)) + lam_init)
    o = acc_ref[...] / l_ref[...]
    a = o[0:tq] - lam * o[tq:2 * tq]
    o_ref[...] = (_rms(a, subln_ref[...]) * (1.0 - lam_init)).astype(o_ref.dtype)


def diff_attention(proj, lam_params, subln, lam_init, batch, seq, tq=256, tk=512):
    nq = seq // tq
    tk = min(tk, seq)
    return pl.pallas_call(
        functools.partial(_diff_attn_kernel, tk=tk, lam_init=lam_init),
        grid=(batch, DIFF_HEADS, nq),
        in_specs=[pl.BlockSpec((tq, HEAD_DIM), lambda b, h, i: (b * nq + i, CB_QB + h)),
                  pl.BlockSpec((seq, HEAD_DIM), lambda b, h, i: (b, CB_KB + h)),
                  pl.BlockSpec((seq, HEAD_DIM), lambda b, h, i: (b, CB_VB + h)),
                  pl.BlockSpec((4, DIFF_QK), lambda b, h, i: (0, 0)),
                  pl.BlockSpec((None, 1, HEAD_DIM), lambda b, h, i: (h, 0, 0))],
        out_specs=pl.BlockSpec((tq, HEAD_DIM), lambda b, h, i: (b * nq + i, h)),
        out_shape=jax.ShapeDtypeStruct((batch * seq, DIFF_HEADS * HEAD_DIM), BF16),
        scratch_shapes=[pltpu.VMEM((2 * tq, 1), F32), pltpu.VMEM((2 * tq, 1), F32),
                        pltpu.VMEM((2 * tq, HEAD_DIM), F32)],
        compiler_params=_cparams("parallel", "parallel", "arbitrary"),
        name="diff_attention",
    )(proj, proj, proj, lam_params, subln)


def _lru_kernel(xb_ref, yb_ref, cw_ref, cb_ref, wa_ref, wx_ref, ba_ref, bx_ref, lam_ref, o_ref,
                xpad_ref, a_ref, u_ref, h_ref, carry_ref):
    tt = xb_ref.shape[0]
    pad = SUBLANES

    @pl.when(pl.program_id(2) == 0)
    def _():
        xpad_ref[0:pad, :] = jnp.zeros((pad, xpad_ref.shape[1]), F32)
        carry_ref[...] = jnp.zeros_like(carry_ref)

    xb = xb_ref[...]
    xpad_ref[pad:pad + tt, :] = xb
    cw = cw_ref[...]
    xc = cb_ref[...] + cw[CONV_W - 1:CONV_W] * xb
    for w in range(CONV_W - 1):
        off = pad - (CONV_W - 1) + w
        xc = xc + cw[w:w + 1] * xpad_ref[off:off + tt, :]
    xpad_ref[0:pad, :] = xb[tt - pad:tt]

    xcb = xc.astype(BF16)
    r = jax.nn.sigmoid(_dot(xcb, wa_ref[...]) + ba_ref[...])
    i = jax.nn.sigmoid(_dot(xcb, wx_ref[...]) + bx_ref[...])
    nl = -lam_ref[...]
    softplus = jnp.maximum(nl, 0.0) + jnp.log1p(jnp.exp(-jnp.abs(nl)))
    log_a = -LRU_C * r * softplus
    a = jnp.exp(log_a)
    a_ref[...] = a
    u_ref[...] = jnp.sqrt(1.0 - a * a) * (i * xc)

    row = lax.broadcasted_iota(jnp.int32, (SUBLANES, a.shape[1]), 0)

    def slab(j, hprev):
        r0 = pl.multiple_of(j * SUBLANES, SUBLANES)
        aa = a_ref[pl.ds(r0, SUBLANES), :]
        bb = u_ref[pl.ds(r0, SUBLANES), :]
        for d in (1, 2, 4):
            a_sh = jnp.where(row >= d, pltpu.roll(aa, shift=d, axis=0), 1.0)
            b_sh = jnp.where(row >= d, pltpu.roll(bb, shift=d, axis=0), 0.0)
            bb = aa * b_sh + bb
            aa = aa * a_sh
        hh = bb + aa * hprev
        h_ref[pl.ds(r0, SUBLANES), :] = hh
        return hh[SUBLANES - 1:SUBLANES, :]

    carry_ref[...] = lax.fori_loop(0, tt // SUBLANES, slab, carry_ref[...], unroll=4)
    o_ref[...] = (h_ref[...] * jax.nn.gelu(yb_ref[...])).astype(o_ref.dtype)


def lru(proj, conv_w, conv_b, wa, wx, ba, bx, lam, batch, seq, tt=512):
    w = proj.shape[1] // 2
    bw = w // LRU_BLOCKS
    nt = seq // tt
    vec = lambda: pl.BlockSpec((1, bw), lambda b, n, t: (0, n))
    return pl.pallas_call(
        _lru_kernel,
        grid=(batch, LRU_BLOCKS, nt),
        in_specs=[pl.BlockSpec((tt, bw), lambda b, n, t: (b * nt + t, n)),
                  pl.BlockSpec((tt, bw), lambda b, n, t: (b * nt + t, LRU_BLOCKS + n)),
                  pl.BlockSpec((CONV_W, bw), lambda b, n, t: (0, n)),
                  vec(),
                  pl.BlockSpec((None, bw, bw), lambda b, n, t: (n, 0, 0)),
                  pl.BlockSpec((None, bw, bw), lambda b, n, t: (n, 0, 0)),
                  vec(), vec(), vec()],
        out_specs=pl.BlockSpec((tt, bw), lambda b, n, t: (b * nt + t, n)),
        out_shape=jax.ShapeDtypeStruct((batch * seq, w), BF16),
        scratch_shapes=[pltpu.VMEM((tt + SUBLANES, bw), F32), pltpu.VMEM((tt, bw), F32),
                        pltpu.VMEM((tt, bw), F32), pltpu.VMEM((tt, bw), F32), pltpu.VMEM((1, bw), F32)],
        compiler_params=_cparams("parallel", "parallel", "arbitrary"),
        name="rglru",
    )(proj, proj, conv_w, conv_b, wa, wx, ba, bx, lam)


def _sel_constants(seq):
    n_blk = seq // CMP_STRIDE
    n_sel = seq // SEL_LEN
    c0 = np.arange(n_blk)[:, None] * CMP_STRIDE
    j0 = np.arange(LANES)[None, :] * SEL_LEN
    ov = np.clip(np.minimum(c0 + CMP_LEN, j0 + SEL_LEN) - np.maximum(c0, j0), 0, None) / CMP_LEN
    ov[n_blk - 1:, :] = 0.0
    ov[:, n_sel:] = 0.0
    expand = (np.arange(seq)[:, None] // SEL_LEN == np.arange(LANES)[None, :]).astype(np.float32)
    return jnp.asarray(ov, BF16), jnp.asarray(expand, BF16)


def _attn_layer(h, layer_idx, g_mix, w_in, pe_k, w1_k, w2_k, pe_v, w1_v, w2_v,
                lq1, lk1, lq2, lk2, subln, w_out, batch, seq):
    a_q = NSA_HEADS * HEAD_DIM
    a_kv = NSA_KV * HEAD_DIM
    o2 = a_q + 6 * a_kv
    o3 = o2 + NSA_HEADS * N_BRANCH
    w_main = jnp.concatenate([w_in[:, :o2], w_in[:, o3:]], axis=1).astype(BF16)
    w_gate = jnp.pad(w_in[:, o2:o3], ((0, 0), (0, LANES - (o3 - o2)))).astype(BF16)
    scale = np.ones((1, N_MAIN_COLS), np.float32)
    scale[:, :a_q] = HEAD_DIM ** -0.5
    scale[:, CB_QB * LANES:CB_KB * LANES] = DIFF_QK ** -0.5
    g = g_mix.reshape(1, -1)
    proj = norm_matmul(h, g, w_main, jnp.asarray(scale), BF16)
    gates = norm_matmul(h, g, w_gate, jnp.ones((1, LANES), F32), F32)

    nb = seq // CMP_STRIDE
    x4 = proj[:, CB_KCMP * LANES:CB_KSEL * LANES].reshape(batch, nb, CMP_STRIDE, 2 * NSA_KV, HEAD_DIM)
    x4 = x4.transpose(0, 3, 1, 2, 4).reshape(batch, 2 * NSA_KV, nb, CMP_STRIDE * HEAD_DIM)
    flat = lambda a: a.reshape(1, CMP_STRIDE * HEAD_DIM)
    pelo = jnp.stack([flat(pe_k[:CMP_STRIDE]), flat(pe_v[:CMP_STRIDE])])
    pehi = jnp.stack([flat(pe_k[CMP_STRIDE:]), flat(pe_v[CMP_STRIDE:])])
    wflat = lambda a: a.reshape(CMP_STRIDE * HEAD_DIM, HEAD_DIM)
    w1lo = jnp.stack([wflat(w1_k[:CMP_STRIDE]), wflat(w1_v[:CMP_STRIDE])]).astype(BF16)
    w1hi = jnp.stack([wflat(w1_k[CMP_STRIDE:]), wflat(w1_v[CMP_STRIDE:])]).astype(BF16)
    w2 = jnp.stack([w2_k, w2_v]).astype(BF16)
    kvc = compress(x4, pelo, pehi, w1lo, w1hi, w2)

    selmap, expand = _sel_constants(seq)
    oc, selmask = nsa_cmp_select(proj, kvc, gates, selmap, batch, seq)
    o_a = nsa_sel_win(proj, selmask, expand, gates, oc, batch, seq)

    lam_init = 0.8 - 0.6 * math.exp(-0.3 * layer_idx)
    lam_params = jnp.stack([lq1, lk1, lq2, lk2]).astype(F32)
    o_b = diff_attention(proj, lam_params, subln.reshape(DIFF_HEADS, 1, HEAD_DIM), lam_init, batch, seq)

    w_out = w_out.astype(BF16)
    return matmul_residual([o_a, o_b], [w_out[:a_q], w_out[a_q:]], h)


def _rec_layer(h, g_mix, w_in, conv_w, conv_b, wa, ba, wx, bx, lam_p, w_out, batch, seq):
    n = w_in.shape[1]
    proj = norm_matmul(h, g_mix.reshape(1, -1), w_in.astype(BF16), jnp.ones((1, n), F32), F32)
    row = lambda v: v.reshape(1, -1)
    hy = lru(proj, conv_w, row(conv_b), wa.astype(BF16), wx.astype(BF16), row(ba), row(bx), row(lam_p), batch, seq)
    return matmul_residual([hy], [w_out.astype(BF16)], h)


def kernel(x, p, g_mix, g_ffn, g_ple, g_final, w_in_attn, cmp_pe_k, cmp_w1_k, cmp_w2_k, cmp_pe_v, cmp_w1_v, cmp_w2_v, diff_lq1, diff_lk1, diff_lq2, diff_lk2, diff_subln, w_out_attn, w_in_rec, conv_w, conv_b, lru_wa, lru_ba, lru_wx, lru_bx, lru_lambda, w_out_rec, w_ffn_gate, w_ffn_up, w_ffn_down, w_ple_proj, w_ple_gate):
    batch, seq, d = x.shape
    depth = p.shape[0]
    h = x.reshape(batch * seq, d)
    for i in range(depth):
        j = i // 2
        if i % 2 == 0:
            h = _attn_layer(h, i, g_mix[i], w_in_attn[j], cmp_pe_k[j], cmp_w1_k[j], cmp_w2_k[j],
                            cmp_pe_v[j], cmp_w1_v[j], cmp_w2_v[j], diff_lq1[j], diff_lk1[j],
                            diff_lq2[j], diff_lk2[j], diff_subln[j], w_out_attn[j], batch, seq)
        else:
            h = _rec_layer(h, g_mix[i], w_in_rec[j], conv_w[j], conv_b[j], lru_wa[j], lru_ba[j],
                           lru_wx[j], lru_bx[j], lru_lambda[j], w_out_rec[j], batch, seq)
        h = ffn(h, g_ffn[i].reshape(1, -1), w_ffn_gate[i].astype(BF16), w_ffn_up[i].astype(BF16),
                w_ffn_down[i].astype(BF16))
        h = ple(h, g_ple[i].reshape(1, -1), p[i].reshape(batch * seq, -1),
                w_ple_gate[i].astype(BF16), w_ple_proj[i].astype(BF16))
    return final_norm(h, g_final.reshape(1, -1)).reshape(batch, seq, d)
```

```python
import functools
import math

import numpy as np
import jax
import jax.numpy as jnp
from jax import lax
from jax.experimental import pallas as pl
from jax.experimental.pallas import tpu as pltpu

F32 = jnp.float32
BF16 = jnp.bfloat16

EPS = 1e-6
NEG_INF = -1e30
BIG = 1e6

NSA_HEADS = 8
NSA_KV = 2
NSA_GROUP = NSA_HEADS // NSA_KV
HEAD_DIM = 128
CMP_STRIDE = 16
CMP_LEN = 2 * CMP_STRIDE
SEL_LEN = 64
SEL_TOPK = 16
WIN = 512
N_BRANCH = 3
DIFF_HEADS = 8
DIFF_QK = 64
CONV_W = 4
LRU_BLOCKS = 8
LRU_C = 8.0

LANES = 128
SUBLANES = 8
VMEM_LIMIT = 56 * 1024 * 1024

CB_QA = 0
CB_KCMP, CB_VCMP = 8, 10
CB_KSEL, CB_VSEL = 12, 14
CB_KWIN, CB_VWIN = 16, 18
CB_QB, CB_KB, CB_VB = 20, 28, 36
N_MAIN_COLS = 44 * LANES

LOG2E = math.log2(math.e)
DIFF_TQ, DIFF_TK = 512, 512
CHAIN_W = 256
NSA_TQ, NSA_TK = 256, 512


def _cparams(*sem):
    return pltpu.CompilerParams(dimension_semantics=sem, vmem_limit_bytes=VMEM_LIMIT)


def _dot(a, b):
    return jnp.dot(a, b, preferred_element_type=F32)


def _dot_nt(a, b):
    return lax.dot_general(a, b, (((1,), (1,)), ((), ())), preferred_element_type=F32)


def _rms(x, g):
    return x * lax.rsqrt(jnp.mean(x * x, axis=-1, keepdims=True) + EPS) * g


def _norm_matmul_kernel(x_ref, g_ref, w_ref, s_ref, o_ref, xn_ref):
    @pl.when(pl.program_id(1) == 0)
    def _():
        xn_ref[...] = _rms(x_ref[...], g_ref[...]).astype(BF16)

    o_ref[...] = (_dot(xn_ref[...], w_ref[...]) * s_ref[...]).astype(o_ref.dtype)


def norm_matmul(x, g, w, col_scale, out_dtype, tm=512, tn=512):
    m, d = x.shape
    n = w.shape[1]
    tn = min(tn, n)
    return pl.pallas_call(
        _norm_matmul_kernel,
        grid=(m // tm, n // tn),
        in_specs=[pl.BlockSpec((tm, d), lambda i, j: (i, 0)),
                  pl.BlockSpec((1, d), lambda i, j: (0, 0)),
                  pl.BlockSpec((d, tn), lambda i, j: (0, j)),
                  pl.BlockSpec((1, tn), lambda i, j: (0, j))],
        out_specs=pl.BlockSpec((tm, tn), lambda i, j: (i, j)),
        out_shape=jax.ShapeDtypeStruct((m, n), out_dtype),
        scratch_shapes=[pltpu.VMEM((tm, d), BF16)],
        compiler_params=_cparams("parallel", "arbitrary"),
        name="norm_matmul",
    )(x, g, w, col_scale)


def _matmul_res_kernel(*refs):
    n_ops = (len(refs) - 2) // 2
    res_ref, o_ref = refs[2 * n_ops], refs[2 * n_ops + 1]
    acc = res_ref[...]
    for a_ref, w_ref in zip(refs[:n_ops], refs[n_ops:2 * n_ops]):
        acc = acc + _dot(a_ref[...], w_ref[...])
    o_ref[...] = acc


def matmul_residual(a_list, w_list, res, tm=512, tn=512):
    m, n = res.shape
    in_specs = [pl.BlockSpec((tm, a.shape[1]), lambda i, j: (i, 0)) for a in a_list]
    in_specs += [pl.BlockSpec((w.shape[0], tn), lambda i, j: (0, j)) for w in w_list]
    in_specs += [pl.BlockSpec((tm, tn), lambda i, j: (i, j))]
    return pl.pallas_call(
        _matmul_res_kernel,
        grid=(m // tm, n // tn),
        in_specs=in_specs,
        out_specs=pl.BlockSpec((tm, tn), lambda i, j: (i, j)),
        out_shape=jax.ShapeDtypeStruct((m, n), F32),
        compiler_params=_cparams("parallel", "arbitrary"),
        name="matmul_residual",
    )(*a_list, *w_list, res)


def _ffn_kernel(x_ref, g_ref, wg_ref, wu_ref, wd_ref, o_ref, xn_ref, acc_ref):
    f = pl.program_id(1)

    @pl.when(f == 0)
    def _():
        xn_ref[...] = _rms(x_ref[...], g_ref[...]).astype(BF16)
        acc_ref[...] = jnp.zeros_like(acc_ref)

    xn = xn_ref[...]
    act = jax.nn.silu(_dot(xn, wg_ref[...])) * _dot(xn, wu_ref[...])
    acc_ref[...] += _dot(act.astype(BF16), wd_ref[...])

    @pl.when(f == pl.num_programs(1) - 1)
    def _():
        o_ref[...] = x_ref[...] + acc_ref[...]


def ffn(x, g, wg, wu, wd, tm=512, tf=512):
    m, d = x.shape
    dff = wg.shape[1]
    return pl.pallas_call(
        _ffn_kernel,
        grid=(m // tm, dff // tf),
        in_specs=[pl.BlockSpec((tm, d), lambda i, f: (i, 0)),
                  pl.BlockSpec((1, d), lambda i, f: (0, 0)),
                  pl.BlockSpec((d, tf), lambda i, f: (0, f)),
                  pl.BlockSpec((d, tf), lambda i, f: (0, f)),
                  pl.BlockSpec((tf, d), lambda i, f: (f, 0))],
        out_specs=pl.BlockSpec((tm, d), lambda i, f: (i, 0)),
        out_shape=jax.ShapeDtypeStruct((m, d), F32),
        scratch_shapes=[pltpu.VMEM((tm, d), BF16), pltpu.VMEM((tm, d), F32)],
        compiler_params=_cparams("parallel", "arbitrary"),
        name="ffn",
    )(x, g, wg, wu, wd)


def _ple_kernel(x_ref, xc_ref, g_ref, p_ref, wg_ref, wp_ref, o_ref, xn_ref):
    @pl.when(pl.program_id(1) == 0)
    def _():
        xn_ref[...] = _rms(x_ref[...], g_ref[...]).astype(BF16)

    gate = jax.nn.sigmoid(_dot(xn_ref[...], wg_ref[...]))
    o_ref[...] = xc_ref[...] + gate * _dot(p_ref[...].astype(BF16), wp_ref[...])


def ple(x, g, p, wgate, wproj, tm=512, tn=512):
    m, d = x.shape
    pd = p.shape[1]
    return pl.pallas_call(
        _ple_kernel,
        grid=(m // tm, d // tn),
        in_specs=[pl.BlockSpec((tm, d), lambda i, j: (i, 0)),
                  pl.BlockSpec((tm, tn), lambda i, j: (i, j)),
                  pl.BlockSpec((1, d), lambda i, j: (0, 0)),
                  pl.BlockSpec((tm, pd), lambda i, j: (i, 0)),
                  pl.BlockSpec((d, tn), lambda i, j: (0, j)),
                  pl.BlockSpec((pd, tn), lambda i, j: (0, j))],
        out_specs=pl.BlockSpec((tm, tn), lambda i, j: (i, j)),
        out_shape=jax.ShapeDtypeStruct((m, d), F32),
        scratch_shapes=[pltpu.VMEM((tm, d), BF16)],
        compiler_params=_cparams("parallel", "arbitrary"),
        name="ple",
    )(x, x, g, p, wgate, wproj)


def _final_norm_kernel(x_ref, g_ref, o_ref):
    o_ref[...] = _rms(x_ref[...], g_ref[...])


def final_norm(x, g, tm=512):
    m, d = x.shape
    return pl.pallas_call(
        _final_norm_kernel,
        grid=(m // tm,),
        in_specs=[pl.BlockSpec((tm, d), lambda i: (i, 0)),
                  pl.BlockSpec((1, d), lambda i: (0, 0))],
        out_specs=pl.BlockSpec((tm, d), lambda i: (i, 0)),
        out_shape=jax.ShapeDtypeStruct((m, d), F32),
        compiler_params=_cparams("parallel"),
        name="final_norm",
    )(x, g)


def _compress_kernel(x_ref, pelo_ref, pehi_ref, w1lo_ref, w1hi_ref, w2_ref, o_ref):
    x = x_ref[...].astype(F32)
    nb = x.shape[0]
    lo = _dot((x + pelo_ref[...]).astype(BF16), w1lo_ref[...])
    hi = _dot((x + pehi_ref[...]).astype(BF16), w1hi_ref[...])
    z = lo + pltpu.roll(hi, shift=nb - 1, axis=0)
    y = _dot(jax.nn.gelu(z).astype(BF16), w2_ref[...])
    row = lax.broadcasted_iota(jnp.int32, y.shape, 0)
    o_ref[...] = jnp.where(row < nb - 1, y, 0.0).astype(o_ref.dtype)


def compress(x4, pelo, pehi, w1lo, w1hi, w2):
    b, _, nb, kd = x4.shape
    sel = lambda bi, c: (c // NSA_KV, 0, 0)
    return pl.pallas_call(
        _compress_kernel,
        grid=(b, 2 * NSA_KV),
        in_specs=[pl.BlockSpec((None, None, nb, kd), lambda bi, c: (bi, c, 0, 0)),
                  pl.BlockSpec((None, 1, kd), sel),
                  pl.BlockSpec((None, 1, kd), sel),
                  pl.BlockSpec((None, kd, HEAD_DIM), sel),
                  pl.BlockSpec((None, kd, HEAD_DIM), sel),
                  pl.BlockSpec((None, HEAD_DIM, HEAD_DIM), sel)],
        out_specs=pl.BlockSpec((None, None, nb, HEAD_DIM), lambda bi, c: (bi, c, 0, 0)),
        out_shape=jax.ShapeDtypeStruct((b, 2 * NSA_KV, nb, HEAD_DIM), BF16),
        compiler_params=_cparams("parallel", "parallel"),
        name="nsa_compress",
    )(x4, pelo, pehi, w1lo, w1hi, w2)


def _stack_heads(q):
    return jnp.concatenate([q[:, g * HEAD_DIM:(g + 1) * HEAD_DIM] for g in range(NSA_GROUP)], axis=0)


def _nsa_cmp_kernel(q_ref, kc_ref, vc_ref, gate_ref, selmap_ref, oc_ref, sel_ref, *, n_cmp, n_sel, topk):
    tq = q_ref.shape[0]
    s0 = pl.program_id(2) * tq
    q4 = _stack_heads(q_ref[...])
    s = _dot_nt(q4, kc_ref[...])
    t1 = s0 + lax.broadcasted_iota(jnp.int32, (tq, 1), 0)
    t = jnp.concatenate([t1] * NSA_GROUP, axis=0)
    n = lax.broadcasted_iota(jnp.int32, (1, s.shape[1]), 1)
    valid = (n * CMP_STRIDE + (CMP_LEN - 1) <= t) & (n < n_cmp)
    sm = jnp.where(valid, s, NEG_INF)
    e = jnp.where(valid, jnp.exp2(sm - jnp.max(sm, axis=-1, keepdims=True)), 0.0)
    l = jnp.sum(e, axis=-1, keepdims=True)
    p = e / jnp.where(l > 0.0, l, 1.0)
    o = _dot(p.astype(BF16), vc_ref[...])

    gates = jax.nn.sigmoid(gate_ref[...])
    lane = lax.broadcasted_iota(jnp.int32, (1, LANES), 1)
    for g in range(NSA_GROUP):
        gsel = jnp.sum(jnp.where(lane == g * N_BRANCH, gates, 0.0), axis=-1, keepdims=True)
        oc_ref[:, g * HEAD_DIM:(g + 1) * HEAD_DIM] = gsel * o[g * tq:(g + 1) * tq]

    ps = p[0:tq]
    for g in range(1, NSA_GROUP):
        ps = ps + p[g * tq:(g + 1) * tq]
    ps_hi = ps.astype(BF16)
    ps_lo = (ps - ps_hi.astype(F32)).astype(BF16)
    imp = _dot(ps_hi, selmap_ref[...]) + _dot(ps_lo, selmap_ref[...])
    cur = t1 // SEL_LEN
    forced = (lane == 0) | (lane == cur) | (lane == cur - 1)
    imp = jnp.where(lane > cur, -BIG, jnp.where(forced, BIG, imp))
    imp = jnp.where(lane < n_sel, imp, -jnp.inf)
    sel = jnp.zeros(imp.shape, F32)
    for _ in range(topk):
        mx = jnp.max(imp, axis=-1, keepdims=True)
        idx = jnp.min(jnp.where(imp == mx, lane, LANES), axis=-1, keepdims=True)
        pick = lane == idx
        sel = jnp.where(pick, 1.0, sel)
        imp = jnp.where(pick, -jnp.inf, imp)
    sel_ref[...] = sel.astype(sel_ref.dtype)


def nsa_cmp_select(proj, kvc, gates, selmap, batch, seq, tq=256):
    nb = kvc.shape[2]
    nq = seq // tq
    n_sel = seq // SEL_LEN
    kern = functools.partial(_nsa_cmp_kernel, n_cmp=nb - 1, n_sel=n_sel, topk=min(SEL_TOPK, n_sel))
    gw = NSA_GROUP * HEAD_DIM
    return pl.pallas_call(
        kern,
        grid=(batch, NSA_KV, nq),
        in_specs=[pl.BlockSpec((tq, gw), lambda b, h, i: (b * nq + i, h)),
                  pl.BlockSpec((None, None, nb, HEAD_DIM), lambda b, h, i: (b, h, 0, 0)),
                  pl.BlockSpec((None, None, nb, HEAD_DIM), lambda b, h, i: (b, NSA_KV + h, 0, 0)),
                  pl.BlockSpec((tq, LANES), lambda b, h, i: (b * nq + i, h)),
                  pl.BlockSpec((nb, LANES), lambda b, h, i: (0, 0))],
        out_specs=[pl.BlockSpec((tq, gw), lambda b, h, i: (b * nq + i, h)),
                   pl.BlockSpec((None, None, tq, LANES), lambda b, h, i: (b, h, i, 0))],
        out_shape=[jax.ShapeDtypeStruct((batch * seq, NSA_HEADS * HEAD_DIM), F32),
                   jax.ShapeDtypeStruct((batch, NSA_KV, seq, LANES), BF16)],
        compiler_params=_cparams("parallel", "parallel", "arbitrary"),
        name="nsa_cmp_select",
    )(proj, kvc, kvc, gates, selmap)


def _flash_update_t(st, vt_blk, m_ref, l_ref, acc_ref, c, valid):
    if valid is not None:
        st = jnp.where(valid, st, NEG_INF)
    m_prev = m_ref[c]
    m_new = jnp.maximum(m_prev, jnp.max(st, axis=0, keepdims=True))
    p = jnp.exp2(st - m_new)
    alpha = jnp.exp2(m_prev - m_new)
    l_ref[c] = alpha * l_ref[c] + jnp.sum(p, axis=0, keepdims=True)
    acc_ref[c] = alpha * acc_ref[c] + _dot(vt_blk, p.astype(BF16))
    m_ref[c] = m_new


def _nsa_sel_win_kernel(q_ref, ks_ref, vst_ref, kw_ref, vwt_ref, sel_ref, expand_ref, gate_ref, oc_ref,
                        o_ref, m_ref, l_ref, acc_ref, *, tk):
    tq = q_ref.shape[0]
    i = pl.program_id(2)
    s0 = i * tq
    q = q_ref[...]
    qh = [q[:, g * HEAD_DIM:(g + 1) * HEAD_DIM] for g in range(NSA_GROUP)]
    t = s0 + lax.broadcasted_iota(jnp.int32, (1, tq), 1)
    sel = sel_ref[...]

    m_ref[...] = jnp.full(m_ref.shape, NEG_INF, F32)
    l_ref[...] = jnp.zeros(l_ref.shape, F32)
    acc_ref[...] = jnp.zeros(acc_ref.shape, F32)

    def step(kb, causal):
        k0 = pl.multiple_of(kb * tk, tk)
        kblk = ks_ref[pl.ds(k0, tk), :]
        vt_blk = vst_ref[kb]
        chosen = _dot_nt(expand_ref[pl.ds(k0, tk), :], sel) > 0.5
        scores = [_dot_nt(kblk, qh[g]) for g in range(NSA_GROUP)]
        if causal:
            chosen = chosen & (k0 + lax.broadcasted_iota(jnp.int32, (tk, 1), 0) <= t)
        for g in range(NSA_GROUP):
            _flash_update_t(scores[g], vt_blk, m_ref, l_ref, acc_ref, g, chosen)

    n_full = (s0 + 1) // tk
    n_all = (s0 + tq - 1) // tk + 1

    def full_body(kb, c):
        step(kb, False)
        return c

    def diag_body(kb, c):
        step(kb, True)
        return c

    lax.fori_loop(0, n_full, full_body, 0)
    lax.fori_loop(n_full, n_all, diag_body, 0)

    nwb = WIN // tq + 1
    j0 = jnp.maximum(i - WIN // tq, 0)
    w0 = pl.multiple_of(j0 * tq, tq)
    kwblk = kw_ref[pl.ds(w0, nwb * tq), :]
    dlt = t - (w0 + lax.broadcasted_iota(jnp.int32, (nwb * tq, 1), 0))
    inwin = (dlt >= 0) & (dlt < WIN)
    wscores = [_dot_nt(kwblk, qh[g]) for g in range(NSA_GROUP)]

    gates_t = jax.nn.sigmoid(gate_ref[...]).T
    for g in range(NSA_GROUP):
        sw = jnp.where(inwin, wscores[g], NEG_INF)
        pw = jnp.exp2(sw - jnp.max(sw, axis=0, keepdims=True))
        lw = jnp.sum(pw, axis=0, keepdims=True)
        pw = pw.astype(BF16)
        o_win = _dot(vwt_ref[j0], pw[0:tq])
        for jj in range(1, nwb):
            o_win = o_win + _dot(vwt_ref[j0 + jj], pw[jj * tq:(jj + 1) * tq])
        r = g * N_BRANCH
        o_t = gates_t[r + 1:r + 2] * (acc_ref[g] / l_ref[g]) + gates_t[r + 2:r + 3] * (o_win / lw)
        cs = slice(g * HEAD_DIM, (g + 1) * HEAD_DIM)
        o_ref[:, cs] = (oc_ref[:, cs] + o_t.T).astype(o_ref.dtype)


def nsa_sel_win(proj, vst, vwt, selmask, expand, gates, oc, batch, seq, tq, tk):
    nq = seq // tq
    gw = NSA_GROUP * HEAD_DIM
    kv_spec = lambda cb: pl.BlockSpec((seq, HEAD_DIM), lambda b, h, i: (b, cb + h))
    vt_spec = lambda blk: pl.BlockSpec((None, None, seq // blk, HEAD_DIM, blk), lambda b, h, i: (b, h, 0, 0, 0))
    return pl.pallas_call(
        functools.partial(_nsa_sel_win_kernel, tk=tk),
        grid=(batch, NSA_KV, nq),
        in_specs=[pl.BlockSpec((tq, gw), lambda b, h, i: (b * nq + i, h)),
                  kv_spec(CB_KSEL), vt_spec(tk), kv_spec(CB_KWIN), vt_spec(tq),
                  pl.BlockSpec((None, None, tq, LANES), lambda b, h, i: (b, h, i, 0)),
                  pl.BlockSpec((seq, LANES), lambda b, h, i: (0, 0)),
                  pl.BlockSpec((tq, LANES), lambda b, h, i: (b * nq + i, h)),
                  pl.BlockSpec((tq, gw), lambda b, h, i: (b * nq + i, h))],
        out_specs=pl.BlockSpec((tq, gw), lambda b, h, i: (b * nq + i, h)),
        out_shape=jax.ShapeDtypeStruct((batch * seq, NSA_HEADS * HEAD_DIM), BF16),
        scratch_shapes=[pltpu.VMEM((NSA_GROUP, 1, tq), F32), pltpu.VMEM((NSA_GROUP, 1, tq), F32),
                        pltpu.VMEM((NSA_GROUP, HEAD_DIM, tq), F32)],
        compiler_params=_cparams("parallel", "parallel", "arbitrary"),
        name="nsa_sel_win",
    )(proj, proj, vst, proj, vwt, selmask, expand, gates, oc)


def _diff_attn_kernel(q_ref, k_ref, vt_ref, lamp_ref, subln_ref, o_ref, m_ref, l_ref, acc_ref, *, tk, lam_init):
    tq = q_ref.shape[0]
    s0 = pl.program_id(2) * tq
    q = q_ref[...]
    lane = lax.broadcasted_iota(jnp.int32, (1, LANES), 1)
    zero = jnp.zeros_like(q)
    qmaps = (jnp.where(lane < DIFF_QK, q, zero), jnp.where(lane >= DIFF_QK, q, zero))
    cw = acc_ref.shape[2]
    nj = tq // cw
    t = s0 + lax.broadcasted_iota(jnp.int32, (1, tq), 1)

    m_ref[...] = jnp.full(m_ref.shape, NEG_INF, F32)
    l_ref[...] = jnp.zeros(l_ref.shape, F32)
    acc_ref[...] = jnp.zeros(acc_ref.shape, F32)

    def step(kb, causal):
        k0 = pl.multiple_of(kb * tk, tk)
        kblk = k_ref[pl.ds(k0, tk), :]
        vt_blk = vt_ref[kb]
        chains = [(c, j) for j in range(nj) for c in range(2)]
        scores = [_dot_nt(kblk, qmaps[c][j * cw:(j + 1) * cw]) for c, j in chains]
        for (c, j), st in zip(chains, scores):
            valid = None
            if causal:
                valid = k0 + lax.broadcasted_iota(jnp.int32, (tk, 1), 0) <= t[:, j * cw:(j + 1) * cw]
            _flash_update_t(st, vt_blk, m_ref, l_ref, acc_ref, c * nj + j, valid)

    n_full = (s0 + 1) // tk
    n_all = (s0 + tq - 1) // tk + 1

    def full_body(kb, c):
        step(kb, False)
        return c

    def diag_body(kb, c):
        step(kb, True)
        return c

    lax.fori_loop(0, n_full, full_body, 0)
    lax.fori_loop(n_full, n_all, diag_body, 0)

    lp = lamp_ref[...]
    lam = (jnp.exp(jnp.sum(lp[0:1] * lp[1:2], axis=-1, keepdims=True))
           - jnp.exp(jnp.sum(lp[2:3] * lp[3:4], axis=-1, keepdims=True)) + lam_init)
    for j in range(nj):
        a = acc_ref[j] / l_ref[j] - lam * (acc_ref[nj + j] / l_ref[nj + j])
        y = a * lax.rsqrt(jnp.mean(a * a, axis=0, keepdims=True) + EPS) * subln_ref[...]
        o_ref[j * cw:(j + 1) * cw, :] = (y * (1.0 - lam_init)).T.astype(o_ref.dtype)


def diff_attention(proj, vt, lam_params, subln, lam_init, batch, seq, tq, tk):
    nq = seq // tq
    cw = min(CHAIN_W, tq)
    return pl.pallas_call(
        functools.partial(_diff_attn_kernel, tk=tk, lam_init=lam_init),
        grid=(batch, DIFF_HEADS, nq),
        in_specs=[pl.BlockSpec((tq, HEAD_DIM), lambda b, h, i: (b * nq + i, CB_QB + h)),
                  pl.BlockSpec((seq, HEAD_DIM), lambda b, h, i: (b, CB_KB + h)),
                  pl.BlockSpec((None, None, seq // tk, HEAD_DIM, tk), lambda b, h, i: (b, h, 0, 0, 0)),
                  pl.BlockSpec((4, DIFF_QK), lambda b, h, i: (0, 0)),
                  pl.BlockSpec((None, HEAD_DIM, 1), lambda b, h, i: (h, 0, 0))],
        out_specs=pl.BlockSpec((tq, HEAD_DIM), lambda b, h, i: (b * nq + i, h)),
        out_shape=jax.ShapeDtypeStruct((batch * seq, DIFF_HEADS * HEAD_DIM), BF16),
        scratch_shapes=[pltpu.VMEM((2 * tq // cw, 1, cw), F32), pltpu.VMEM((2 * tq // cw, 1, cw), F32),
                        pltpu.VMEM((2 * tq // cw, HEAD_DIM, cw), F32)],
        compiler_params=_cparams("parallel", "parallel", "arbitrary"),
        name="diff_attention",
    )(proj, proj, vt, lam_params, subln)


def _lru_kernel(xb_ref, yb_ref, cw_ref, cb_ref, wa_ref, wx_ref, ba_ref, bx_ref, lam_ref, o_ref,
                xpad_ref, a_ref, u_ref, h_ref, carry_ref):
    tt = xb_ref.shape[0]
    pad = SUBLANES

    @pl.when(pl.program_id(2) == 0)
    def _():
        xpad_ref[0:pad, :] = jnp.zeros((pad, xpad_ref.shape[1]), F32)
        carry_ref[...] = jnp.zeros_like(carry_ref)

    xb = xb_ref[...]
    xpad_ref[pad:pad + tt, :] = xb
    cw = cw_ref[...]
    xc = cb_ref[...] + cw[CONV_W - 1:CONV_W] * xb
    for w in range(CONV_W - 1):
        off = pad - (CONV_W - 1) + w
        xc = xc + cw[w:w + 1] * xpad_ref[off:off + tt, :]
    xpad_ref[0:pad, :] = xb[tt - pad:tt]

    xcb = xc.astype(BF16)
    r = jax.nn.sigmoid(_dot(xcb, wa_ref[...]) + ba_ref[...])
    i = jax.nn.sigmoid(_dot(xcb, wx_ref[...]) + bx_ref[...])
    nl = -lam_ref[...]
    softplus = jnp.maximum(nl, 0.0) + jnp.log1p(jnp.exp(-jnp.abs(nl)))
    log_a = -LRU_C * r * softplus
    a = jnp.exp(log_a)
    a_ref[...] = a
    u_ref[...] = jnp.sqrt(1.0 - a * a) * (i * xc)

    row = lax.broadcasted_iota(jnp.int32, (SUBLANES, a.shape[1]), 0)

    def slab(j, hprev):
        r0 = pl.multiple_of(j * SUBLANES, SUBLANES)
        aa = a_ref[pl.ds(r0, SUBLANES), :]
        bb = u_ref[pl.ds(r0, SUBLANES), :]
        for d in (1, 2, 4):
            a_sh = jnp.where(row >= d, pltpu.roll(aa, shift=d, axis=0), 1.0)
            b_sh = jnp.where(row >= d, pltpu.roll(bb, shift=d, axis=0), 0.0)
            bb = aa * b_sh + bb
            aa = aa * a_sh
        hh = bb + aa * hprev
        h_ref[pl.ds(r0, SUBLANES), :] = hh
        return hh[SUBLANES - 1:SUBLANES, :]

    carry_ref[...] = lax.fori_loop(0, tt // SUBLANES, slab, carry_ref[...], unroll=4)
    o_ref[...] = (h_ref[...] * jax.nn.gelu(yb_ref[...])).astype(o_ref.dtype)


def lru(proj, conv_w, conv_b, wa, wx, ba, bx, lam, batch, seq, tt=512):
    w = proj.shape[1] // 2
    bw = w // LRU_BLOCKS
    nt = seq // tt
    vec = lambda: pl.BlockSpec((1, bw), lambda b, n, t: (0, n))
    return pl.pallas_call(
        _lru_kernel,
        grid=(batch, LRU_BLOCKS, nt),
        in_specs=[pl.BlockSpec((tt, bw), lambda b, n, t: (b * nt + t, n)),
                  pl.BlockSpec((tt, bw), lambda b, n, t: (b * nt + t, LRU_BLOCKS + n)),
                  pl.BlockSpec((CONV_W, bw), lambda b, n, t: (0, n)),
                  vec(),
                  pl.BlockSpec((None, bw, bw), lambda b, n, t: (n, 0, 0)),
                  pl.BlockSpec((None, bw, bw), lambda b, n, t: (n, 0, 0)),
                  vec(), vec(), vec()],
        out_specs=pl.BlockSpec((tt, bw), lambda b, n, t: (b * nt + t, n)),
        out_shape=jax.ShapeDtypeStruct((batch * seq, w), BF16),
        scratch_shapes=[pltpu.VMEM((tt + SUBLANES, bw), F32), pltpu.VMEM((tt, bw), F32),
                        pltpu.VMEM((tt, bw), F32), pltpu.VMEM((tt, bw), F32), pltpu.VMEM((1, bw), F32)],
        compiler_params=_cparams("parallel", "parallel", "arbitrary"),
        name="rglru",
    )(proj, proj, conv_w, conv_b, wa, wx, ba, bx, lam)


def _sel_constants(seq):
    n_blk = seq // CMP_STRIDE
    n_sel = seq // SEL_LEN
    c0 = np.arange(n_blk)[:, None] * CMP_STRIDE
    j0 = np.arange(LANES)[None, :] * SEL_LEN
    ov = np.clip(np.minimum(c0 + CMP_LEN, j0 + SEL_LEN) - np.maximum(c0, j0), 0, None) / CMP_LEN
    ov[n_blk - 1:, :] = 0.0
    ov[:, n_sel:] = 0.0
    expand = (np.arange(seq)[:, None] // SEL_LEN == np.arange(LANES)[None, :]).astype(np.float32)
    return jnp.asarray(ov, BF16), jnp.asarray(expand, BF16)


def _transposed_blocks(proj, col_block, n_heads, batch, seq, blk):
    v = proj[:, col_block * LANES:(col_block + n_heads) * LANES].reshape(batch, seq // blk, blk, n_heads, HEAD_DIM)
    return v.transpose(0, 3, 1, 4, 2)


def _attn_layer(h, layer_idx, g_mix, w_in, pe_k, w1_k, w2_k, pe_v, w1_v, w2_v,
                lq1, lk1, lq2, lk2, subln, w_out, batch, seq):
    a_q = NSA_HEADS * HEAD_DIM
    a_kv = NSA_KV * HEAD_DIM
    o2 = a_q + 6 * a_kv
    o3 = o2 + NSA_HEADS * N_BRANCH
    w_main = jnp.concatenate([w_in[:, :o2], w_in[:, o3:]], axis=1).astype(BF16)
    ng = NSA_GROUP * N_BRANCH
    w_gate = jnp.concatenate([jnp.pad(w_in[:, o2 + k * ng:o2 + (k + 1) * ng], ((0, 0), (0, LANES - ng)))
                              for k in range(NSA_KV)], axis=1).astype(BF16)
    scale = np.ones((1, N_MAIN_COLS), np.float32)
    scale[:, :a_q] = HEAD_DIM ** -0.5 * LOG2E
    scale[:, CB_QB * LANES:CB_KB * LANES] = DIFF_QK ** -0.5 * LOG2E
    g = g_mix.reshape(1, -1)
    proj = norm_matmul(h, g, w_main, jnp.asarray(scale), BF16)
    gates = norm_matmul(h, g, w_gate, jnp.ones((1, NSA_KV * LANES), F32), F32, tn=LANES)

    nb = seq // CMP_STRIDE
    x4 = proj[:, CB_KCMP * LANES:CB_KSEL * LANES].reshape(batch, nb, CMP_STRIDE, 2 * NSA_KV, HEAD_DIM)
    x4 = x4.transpose(0, 3, 1, 2, 4).reshape(batch, 2 * NSA_KV, nb, CMP_STRIDE * HEAD_DIM)
    flat = lambda a: a.reshape(1, CMP_STRIDE * HEAD_DIM)
    pelo = jnp.stack([flat(pe_k[:CMP_STRIDE]), flat(pe_v[:CMP_STRIDE])])
    pehi = jnp.stack([flat(pe_k[CMP_STRIDE:]), flat(pe_v[CMP_STRIDE:])])
    wflat = lambda a: a.reshape(CMP_STRIDE * HEAD_DIM, HEAD_DIM)
    w1lo = jnp.stack([wflat(w1_k[:CMP_STRIDE]), wflat(w1_v[:CMP_STRIDE])]).astype(BF16)
    w1hi = jnp.stack([wflat(w1_k[CMP_STRIDE:]), wflat(w1_v[CMP_STRIDE:])]).astype(BF16)
    w2 = jnp.stack([w2_k, w2_v]).astype(BF16)
    kvc = compress(x4, pelo, pehi, w1lo, w1hi, w2)

    selmap, expand = _sel_constants(seq)
    oc, selmask = nsa_cmp_select(proj, kvc, gates, selmap, batch, seq)
    tq, tk = min(NSA_TQ, seq), min(NSA_TK, seq)
    vst = _transposed_blocks(proj, CB_VSEL, NSA_KV, batch, seq, tk)
    vwt = _transposed_blocks(proj, CB_VWIN, NSA_KV, batch, seq, tq)
    o_a = nsa_sel_win(proj, vst, vwt, selmask, expand, gates, oc, batch, seq, tq, tk)

    lam_init = 0.8 - 0.6 * math.exp(-0.3 * layer_idx)
    lam_params = jnp.stack([lq1, lk1, lq2, lk2]).astype(F32)
    tk = min(DIFF_TK, seq)
    vt = _transposed_blocks(proj, CB_VB, DIFF_HEADS, batch, seq, tk)
    o_b = diff_attention(proj, vt, lam_params, subln.reshape(DIFF_HEADS, HEAD_DIM, 1), lam_init, batch, seq,
                         min(DIFF_TQ, seq), tk)

    w_out = w_out.astype(BF16)
    return matmul_residual([o_a, o_b], [w_out[:a_q], w_out[a_q:]], h)


def _rec_layer(h, g_mix, w_in, conv_w, conv_b, wa, ba, wx, bx, lam_p, w_out, batch, seq):
    n = w_in.shape[1]
    proj = norm_matmul(h, g_mix.reshape(1, -1), w_in.astype(BF16), jnp.ones((1, n), F32), F32)
    row = lambda v: v.reshape(1, -1)
    hy = lru(proj, conv_w, row(conv_b), wa.astype(BF16), wx.astype(BF16), row(ba), row(bx), row(lam_p), batch, seq)
    return matmul_residual([hy], [w_out.astype(BF16)], h)


def kernel(x, p, g_mix, g_ffn, g_ple, g_final, w_in_attn, cmp_pe_k, cmp_w1_k, cmp_w2_k, cmp_pe_v, cmp_w1_v, cmp_w2_v, diff_lq1, diff_lk1, diff_lq2, diff_lk2, diff_subln, w_out_attn, w_in_rec, conv_w, conv_b, lru_wa, lru_ba, lru_wx, lru_bx, lru_lambda, w_out_rec, w_ffn_gate, w_ffn_up, w_ffn_down, w_ple_proj, w_ple_gate):
    batch, seq, d = x.shape
    depth = p.shape[0]
    h = x.reshape(batch * seq, d)
    for i in range(depth):
        j = i // 2
        if i % 2 == 0:
            h = _attn_layer(h, i, g_mix[i], w_in_attn[j], cmp_pe_k[j], cmp_w1_k[j], cmp_w2_k[j],
                            cmp_pe_v[j], cmp_w1_v[j], cmp_w2_v[j], diff_lq1[j], diff_lk1[j],
                            diff_lq2[j], diff_lk2[j], diff_subln[j], w_out_attn[j], batch, seq)
        else:
            h = _rec_layer(h, g_mix[i], w_in_rec[j], conv_w[j], conv_b[j], lru_wa[j], lru_ba[j],
                           lru_wx[j], lru_bx[j], lru_lambda[j], w_out_rec[j], batch, seq)
        h = ffn(h, g_ffn[i].reshape(1, -1), w_ffn_gate[i].astype(BF16), w_ffn_up[i].astype(BF16),
                w_ffn_down[i].astype(BF16))
        h = ple(h, g_ple[i].reshape(1, -1), p[i].reshape(batch * seq, -1),
                w_ple_gate[i].astype(BF16), w_ple_proj[i].astype(BF16))
    return final_norm(h, g_final.reshape(1, -1)).reshape(batch, seq, d)
```

```python
import functools
import math

import numpy as np
import jax
import jax.numpy as jnp
from jax import lax
from jax.experimental import pallas as pl
from jax.experimental.pallas import tpu as pltpu

F32 = jnp.float32
BF16 = jnp.bfloat16

EPS = 1e-6
NEG_INF = -1e30
BIG = 1e6

NSA_HEADS = 8
NSA_KV = 2
NSA_GROUP = NSA_HEADS // NSA_KV
HEAD_DIM = 128
CMP_STRIDE = 16
CMP_LEN = 2 * CMP_STRIDE
SEL_LEN = 64
SEL_TOPK = 16
WIN = 512
N_BRANCH = 3
DIFF_HEADS = 8
DIFF_QK = 64
CONV_W = 4
LRU_BLOCKS = 8
LRU_C = 8.0

LANES = 128
SUBLANES = 8
VMEM_LIMIT = 56 * 1024 * 1024

CB_QA = 0
CB_KCMP, CB_VCMP = 8, 10
CB_KSEL, CB_VSEL = 12, 14
CB_KWIN, CB_VWIN = 16, 18
CB_QB, CB_KB, CB_VB = 20, 28, 36
N_MAIN_COLS = 44 * LANES

LOG2E = math.log2(math.e)
DIFF_TQ, DIFF_TK = 1024, 512
CHAIN_W = 256
NSA_TQ, NSA_TK = 512, 512


def _cparams(*sem):
    return pltpu.CompilerParams(dimension_semantics=sem, vmem_limit_bytes=VMEM_LIMIT)


def _dot(a, b):
    return jnp.dot(a, b, preferred_element_type=F32)


def _dot_nt(a, b):
    return lax.dot_general(a, b, (((1,), (1,)), ((), ())), preferred_element_type=F32)


def _rms(x, g):
    return x * lax.rsqrt(jnp.mean(x * x, axis=-1, keepdims=True) + EPS) * g


def _norm_matmul_kernel(x_ref, g_ref, w_ref, s_ref, *rest):
    if len(rest) == 4:
        w_side_ref, o_ref, o_side_ref, xn_ref = rest
    else:
        (o_ref, xn_ref), w_side_ref, o_side_ref = rest, None, None

    @pl.when(pl.program_id(1) == 0)
    def _():
        xn_ref[...] = _rms(x_ref[...], g_ref[...]).astype(BF16)
        if w_side_ref is not None:
            o_side_ref[...] = _dot(xn_ref[...], w_side_ref[...])

    o_ref[...] = (_dot(xn_ref[...], w_ref[...]) * s_ref[...]).astype(o_ref.dtype)


def norm_matmul(x, g, w, col_scale, out_dtype, w_side=None, tm=1024, tn=512):
    m, d = x.shape
    n = w.shape[1]
    tm, tn = min(tm, m), min(tn, n)
    in_specs = [pl.BlockSpec((tm, d), lambda i, j: (i, 0)),
                pl.BlockSpec((1, d), lambda i, j: (0, 0)),
                pl.BlockSpec((d, tn), lambda i, j: (0, j)),
                pl.BlockSpec((1, tn), lambda i, j: (0, j))]
    out_specs = [pl.BlockSpec((tm, tn), lambda i, j: (i, j))]
    out_shape = [jax.ShapeDtypeStruct((m, n), out_dtype)]
    args = [x, g, w, col_scale]
    if w_side is not None:
        n2 = w_side.shape[1]
        in_specs.append(pl.BlockSpec((d, n2), lambda i, j: (0, 0)))
        out_specs.append(pl.BlockSpec((tm, n2), lambda i, j: (i, 0)))
        out_shape.append(jax.ShapeDtypeStruct((m, n2), F32))
        args.append(w_side)
    out = pl.pallas_call(
        _norm_matmul_kernel,
        grid=(m // tm, n // tn),
        in_specs=in_specs,
        out_specs=out_specs,
        out_shape=out_shape,
        scratch_shapes=[pltpu.VMEM((tm, d), BF16)],
        compiler_params=_cparams("parallel", "arbitrary"),
        name="norm_matmul",
    )(*args)
    return out if w_side is not None else out[0]


def _matmul_res_kernel(*refs):
    n_ops = (len(refs) - 2) // 2
    res_ref, o_ref = refs[2 * n_ops], refs[2 * n_ops + 1]
    acc = res_ref[...]
    for a_ref, w_ref in zip(refs[:n_ops], refs[n_ops:2 * n_ops]):
        acc = acc + _dot(a_ref[...], w_ref[...])
    o_ref[...] = acc


def matmul_residual(a_list, w_list, res, tm=1024, tn=512):
    m, n = res.shape
    tm = min(tm, m)
    in_specs = [pl.BlockSpec((tm, a.shape[1]), lambda i, j: (i, 0)) for a in a_list]
    in_specs += [pl.BlockSpec((w.shape[0], tn), lambda i, j: (0, j)) for w in w_list]
    in_specs += [pl.BlockSpec((tm, tn), lambda i, j: (i, j))]
    return pl.pallas_call(
        _matmul_res_kernel,
        grid=(m // tm, n // tn),
        in_specs=in_specs,
        out_specs=pl.BlockSpec((tm, tn), lambda i, j: (i, j)),
        out_shape=jax.ShapeDtypeStruct((m, n), F32),
        compiler_params=_cparams("parallel", "arbitrary"),
        name="matmul_residual",
    )(*a_list, *w_list, res)


def _ffn_kernel(x_ref, g_ref, wg_ref, wu_ref, wd_ref, o_ref, xn_ref, acc_ref):
    f = pl.program_id(1)

    @pl.when(f == 0)
    def _():
        xn_ref[...] = _rms(x_ref[...], g_ref[...]).astype(BF16)
        acc_ref[...] = jnp.zeros_like(acc_ref)

    xn = xn_ref[...]
    act = jax.nn.silu(_dot(xn, wg_ref[...])) * _dot(xn, wu_ref[...])
    acc_ref[...] += _dot(act.astype(BF16), wd_ref[...])

    @pl.when(f == pl.num_programs(1) - 1)
    def _():
        o_ref[...] = x_ref[...] + acc_ref[...]


def ffn(x, g, wg, wu, wd, tm=512, tf=512):
    m, d = x.shape
    dff = wg.shape[1]
    return pl.pallas_call(
        _ffn_kernel,
        grid=(m // tm, dff // tf),
        in_specs=[pl.BlockSpec((tm, d), lambda i, f: (i, 0)),
                  pl.BlockSpec((1, d), lambda i, f: (0, 0)),
                  pl.BlockSpec((d, tf), lambda i, f: (0, f)),
                  pl.BlockSpec((d, tf), lambda i, f: (0, f)),
                  pl.BlockSpec((tf, d), lambda i, f: (f, 0))],
        out_specs=pl.BlockSpec((tm, d), lambda i, f: (i, 0)),
        out_shape=jax.ShapeDtypeStruct((m, d), F32),
        scratch_shapes=[pltpu.VMEM((tm, d), BF16), pltpu.VMEM((tm, d), F32)],
        compiler_params=_cparams("parallel", "arbitrary"),
        name="ffn",
    )(x, g, wg, wu, wd)


def _ple_kernel(x_ref, xc_ref, g_ref, p_ref, wg_ref, wp_ref, o_ref, xn_ref):
    @pl.when(pl.program_id(1) == 0)
    def _():
        xn_ref[...] = _rms(x_ref[...], g_ref[...]).astype(BF16)

    gate = jax.nn.sigmoid(_dot(xn_ref[...], wg_ref[...]))
    o_ref[...] = xc_ref[...] + gate * _dot(p_ref[...].astype(BF16), wp_ref[...])


def ple(x, g, p, wgate, wproj, tm=1024, tn=512):
    m, d = x.shape
    tm = min(tm, m)
    pd = p.shape[1]
    return pl.pallas_call(
        _ple_kernel,
        grid=(m // tm, d // tn),
        in_specs=[pl.BlockSpec((tm, d), lambda i, j: (i, 0)),
                  pl.BlockSpec((tm, tn), lambda i, j: (i, j)),
                  pl.BlockSpec((1, d), lambda i, j: (0, 0)),
                  pl.BlockSpec((tm, pd), lambda i, j: (i, 0)),
                  pl.BlockSpec((d, tn), lambda i, j: (0, j)),
                  pl.BlockSpec((pd, tn), lambda i, j: (0, j))],
        out_specs=pl.BlockSpec((tm, tn), lambda i, j: (i, j)),
        out_shape=jax.ShapeDtypeStruct((m, d), F32),
        scratch_shapes=[pltpu.VMEM((tm, d), BF16)],
        compiler_params=_cparams("parallel", "arbitrary"),
        name="ple",
    )(x, x, g, p, wgate, wproj)


def _final_norm_kernel(x_ref, g_ref, o_ref):
    o_ref[...] = _rms(x_ref[...], g_ref[...])


def final_norm(x, g, tm=512):
    m, d = x.shape
    return pl.pallas_call(
        _final_norm_kernel,
        grid=(m // tm,),
        in_specs=[pl.BlockSpec((tm, d), lambda i: (i, 0)),
                  pl.BlockSpec((1, d), lambda i: (0, 0))],
        out_specs=pl.BlockSpec((tm, d), lambda i: (i, 0)),
        out_shape=jax.ShapeDtypeStruct((m, d), F32),
        compiler_params=_cparams("parallel"),
        name="final_norm",
    )(x, g)


def _compress_kernel(x_ref, pelo_ref, pehi_ref, w1lo_ref, w1hi_ref, w2_ref, o_ref):
    x = x_ref[...].astype(F32)
    nb = x.shape[0]
    lo = _dot((x + pelo_ref[...]).astype(BF16), w1lo_ref[...])
    hi = _dot((x + pehi_ref[...]).astype(BF16), w1hi_ref[...])
    z = lo + pltpu.roll(hi, shift=nb - 1, axis=0)
    y = _dot(jax.nn.gelu(z).astype(BF16), w2_ref[...])
    row = lax.broadcasted_iota(jnp.int32, y.shape, 0)
    o_ref[...] = jnp.where(row < nb - 1, y, 0.0).astype(o_ref.dtype)


def compress(x4, pelo, pehi, w1lo, w1hi, w2):
    b, _, nb, kd = x4.shape
    sel = lambda bi, c: (c // NSA_KV, 0, 0)
    return pl.pallas_call(
        _compress_kernel,
        grid=(b, 2 * NSA_KV),
        in_specs=[pl.BlockSpec((None, None, nb, kd), lambda bi, c: (bi, c, 0, 0)),
                  pl.BlockSpec((None, 1, kd), sel),
                  pl.BlockSpec((None, 1, kd), sel),
                  pl.BlockSpec((None, kd, HEAD_DIM), sel),
                  pl.BlockSpec((None, kd, HEAD_DIM), sel),
                  pl.BlockSpec((None, HEAD_DIM, HEAD_DIM), sel)],
        out_specs=pl.BlockSpec((None, None, nb, HEAD_DIM), lambda bi, c: (bi, c, 0, 0)),
        out_shape=jax.ShapeDtypeStruct((b, 2 * NSA_KV, nb, HEAD_DIM), BF16),
        compiler_params=_cparams("parallel", "parallel"),
        name="nsa_compress",
    )(x4, pelo, pehi, w1lo, w1hi, w2)


def _stack_heads(q):
    return jnp.concatenate([q[:, g * HEAD_DIM:(g + 1) * HEAD_DIM] for g in range(NSA_GROUP)], axis=0)


def _nsa_cmp_kernel(q_ref, kc_ref, vct_ref, gate_ref, selmap_t_ref, oc_ref, sel_t_ref, *, n_cmp, n_sel, topk):
    tq = q_ref.shape[0]
    s0 = pl.program_id(2) * tq
    q4 = _stack_heads(q_ref[...])
    st = _dot_nt(kc_ref[...], q4)
    t1 = s0 + lax.broadcasted_iota(jnp.int32, (1, tq), 1)
    t = jnp.concatenate([t1] * NSA_GROUP, axis=1)
    n = lax.broadcasted_iota(jnp.int32, (st.shape[0], 1), 0)
    valid = (n * CMP_STRIDE + (CMP_LEN - 1) <= t) & (n < n_cmp)
    sm = jnp.where(valid, st, NEG_INF)
    e = jnp.where(valid, jnp.exp2(sm - jnp.max(sm, axis=0, keepdims=True)), 0.0)
    l = jnp.sum(e, axis=0, keepdims=True)
    p = e * (1.0 / jnp.where(l > 0.0, l, 1.0))
    o_t = _dot(vct_ref[...], p.astype(BF16))

    gates_t = jax.nn.sigmoid(gate_ref[...]).T
    for g in range(NSA_GROUP):
        r = g * N_BRANCH
        oc_ref[:, g * HEAD_DIM:(g + 1) * HEAD_DIM] = (gates_t[r:r + 1] * o_t[:, g * tq:(g + 1) * tq]).T

    ps = p[:, 0:tq]
    for g in range(1, NSA_GROUP):
        ps = ps + p[:, g * tq:(g + 1) * tq]
    ps_hi = ps.astype(BF16)
    ps_lo = (ps - ps_hi.astype(F32)).astype(BF16)
    imp = _dot(selmap_t_ref[...], ps_hi) + _dot(selmap_t_ref[...], ps_lo)
    blk = lax.broadcasted_iota(jnp.int32, (LANES, 1), 0)
    cur = t1 // SEL_LEN
    forced = (blk == 0) | (blk == cur) | (blk == cur - 1)
    imp = jnp.where(blk > cur, -BIG, jnp.where(forced, BIG, imp))
    imp = jnp.where(blk < n_sel, imp, -jnp.inf)
    sel = jnp.zeros(imp.shape, F32)
    blk_f = blk.astype(F32)
    for _ in range(topk):
        mx = jnp.max(imp, axis=0, keepdims=True)
        idx = jnp.min(jnp.where(imp == mx, blk_f, float(LANES)), axis=0, keepdims=True)
        pick = blk_f == idx
        sel = jnp.where(pick, 1.0, sel)
        imp = jnp.where(pick, -jnp.inf, imp)
    sel_t_ref[...] = sel.astype(sel_t_ref.dtype)


def nsa_cmp_select(proj, kvc, vct, gates, selmap_t, batch, seq, tq=256):
    nb = kvc.shape[2]
    tq = min(tq, seq)
    nq = seq // tq
    n_sel = seq // SEL_LEN
    kern = functools.partial(_nsa_cmp_kernel, n_cmp=nb - 1, n_sel=n_sel, topk=min(SEL_TOPK, n_sel))
    gw = NSA_GROUP * HEAD_DIM
    return pl.pallas_call(
        kern,
        grid=(batch, NSA_KV, nq),
        in_specs=[pl.BlockSpec((tq, gw), lambda b, h, i: (b * nq + i, h)),
                  pl.BlockSpec((None, None, nb, HEAD_DIM), lambda b, h, i: (b, h, 0, 0)),
                  pl.BlockSpec((None, None, HEAD_DIM, nb), lambda b, h, i: (b, h, 0, 0)),
                  pl.BlockSpec((tq, LANES), lambda b, h, i: (b * nq + i, h)),
                  pl.BlockSpec((LANES, nb), lambda b, h, i: (0, 0))],
        out_specs=[pl.BlockSpec((tq, gw), lambda b, h, i: (b * nq + i, h)),
                   pl.BlockSpec((None, None, LANES, tq), lambda b, h, i: (b, h, 0, i))],
        out_shape=[jax.ShapeDtypeStruct((batch * seq, NSA_HEADS * HEAD_DIM), F32),
                   jax.ShapeDtypeStruct((batch, NSA_KV, LANES, seq), BF16)],
        compiler_params=_cparams("parallel", "parallel", "arbitrary"),
        name="nsa_cmp_select",
    )(proj, kvc, vct, gates, selmap_t)


def _flash_update_t(st, vt_blk, m_ref, l_ref, acc_ref, c, valid):
    if valid is not None:
        st = jnp.where(valid, st, NEG_INF)
    m_prev = m_ref[c]
    m_new = jnp.maximum(m_prev, jnp.max(st, axis=0, keepdims=True))
    p = jnp.exp2(st - m_new)
    alpha = jnp.exp2(m_prev - m_new)
    l_ref[c] = alpha * l_ref[c] + jnp.sum(p, axis=0, keepdims=True)
    acc_ref[c] = alpha * acc_ref[c] + _dot(vt_blk, p.astype(BF16))
    m_ref[c] = m_new


def _causal_chains(diag, nj, cw, tk):
    if diag is None:
        return [(j, False) for j in range(nj)]
    lo, hi = diag * tk, (diag + 1) * tk - 1
    return [(j, hi > j * cw) for j in range(nj) if lo <= (j + 1) * cw - 1]


def _nsa_sel_win_kernel(q_ref, ks_ref, vst_ref, kw_ref, vwt_ref, sel_ref, expand_ref, gate_ref, oc_ref,
                        o_ref, m_ref, l_ref, acc_ref, *, tk):
    tq = q_ref.shape[0]
    cw = acc_ref.shape[2]
    nj = tq // cw
    s0 = pl.program_id(2) * tq
    q = q_ref[...]
    qc = [[q[j * cw:(j + 1) * cw, g * HEAD_DIM:(g + 1) * HEAD_DIM] for j in range(nj)] for g in range(NSA_GROUP)]
    t = s0 + lax.broadcasted_iota(jnp.int32, (1, tq), 1)
    sel_t = sel_ref[...]

    m_ref[...] = jnp.full(m_ref.shape, NEG_INF, F32)
    l_ref[...] = jnp.zeros(l_ref.shape, F32)
    acc_ref[...] = jnp.zeros(acc_ref.shape, F32)

    def step(kb, diag):
        k0 = pl.multiple_of(kb * tk, tk)
        kblk = ks_ref[pl.ds(k0, tk), :]
        vt_blk = vst_ref[kb]
        eblk = expand_ref[pl.ds(k0, tk), :]
        todo = _causal_chains(diag, nj, cw, tk)
        chosen = {j: _dot(eblk, sel_t[:, j * cw:(j + 1) * cw]) > 0.5 for j, _ in todo}
        scores = {(g, j): _dot_nt(kblk, qc[g][j]) for j, _ in todo for g in range(NSA_GROUP)}
        for j, masked in todo:
            valid = chosen[j]
            if masked:
                valid = valid & (k0 + lax.broadcasted_iota(jnp.int32, (tk, 1), 0) <= t[:, j * cw:(j + 1) * cw])
            for g in range(NSA_GROUP):
                _flash_update_t(scores[(g, j)], vt_blk, m_ref, l_ref, acc_ref, g * nj + j, valid)

    n_full = s0 // tk

    def full_body(kb, c):
        step(kb, None)
        return c

    lax.fori_loop(0, n_full, full_body, 0)
    for d in range(tq // tk):
        step(n_full + d, d)

    nwb = WIN // cw + 1
    gates_t = jax.nn.sigmoid(gate_ref[...]).T
    for j in range(nj):
        j0 = jnp.maximum(s0 // cw + j - WIN // cw, 0)
        w0 = pl.multiple_of(j0 * cw, cw)
        kwblk = kw_ref[pl.ds(w0, nwb * cw), :]
        tj = t[:, j * cw:(j + 1) * cw]
        dlt = tj - (w0 + lax.broadcasted_iota(jnp.int32, (nwb * cw, 1), 0))
        inwin = (dlt >= 0) & (dlt < WIN)
        wscores = [_dot_nt(kwblk, qc[g][j]) for g in range(NSA_GROUP)]
        for g in range(NSA_GROUP):
            sw = jnp.where(inwin, wscores[g], NEG_INF)
            pw = jnp.exp2(sw - jnp.max(sw, axis=0, keepdims=True))
            lw = jnp.sum(pw, axis=0, keepdims=True)
            pw = pw.astype(BF16)
            o_win = _dot(vwt_ref[j0], pw[0:cw])
            for jj in range(1, nwb):
                o_win = o_win + _dot(vwt_ref[j0 + jj], pw[jj * cw:(jj + 1) * cw])
            r = g * N_BRANCH
            c = g * nj + j
            g_sel = gates_t[r + 1:r + 2, j * cw:(j + 1) * cw]
            g_win = gates_t[r + 2:r + 3, j * cw:(j + 1) * cw]
            o_t = g_sel * (acc_ref[c] / l_ref[c]) + g_win * (o_win / lw)
            rs, cs = slice(j * cw, (j + 1) * cw), slice(g * HEAD_DIM, (g + 1) * HEAD_DIM)
            o_ref[rs, cs] = (oc_ref[rs, cs] + o_t.T).astype(o_ref.dtype)


def nsa_sel_win(proj, vst, vwt, selmask, expand, gates, oc, batch, seq, tq, tk):
    nq = seq // tq
    cw = vwt.shape[-1]
    nch = NSA_GROUP * tq // cw
    gw = NSA_GROUP * HEAD_DIM
    kv_spec = lambda cb: pl.BlockSpec((seq, HEAD_DIM), lambda b, h, i: (b, cb + h))
    vt_spec = lambda blk: pl.BlockSpec((None, None, seq // blk, HEAD_DIM, blk), lambda b, h, i: (b, h, 0, 0, 0))
    return pl.pallas_call(
        functools.partial(_nsa_sel_win_kernel, tk=tk),
        grid=(batch, NSA_KV, nq),
        in_specs=[pl.BlockSpec((tq, gw), lambda b, h, i: (b * nq + i, h)),
                  kv_spec(CB_KSEL), vt_spec(tk), kv_spec(CB_KWIN), vt_spec(cw),
                  pl.BlockSpec((None, None, LANES, tq), lambda b, h, i: (b, h, 0, i)),
                  pl.BlockSpec((seq, LANES), lambda b, h, i: (0, 0)),
                  pl.BlockSpec((tq, LANES), lambda b, h, i: (b * nq + i, h)),
                  pl.BlockSpec((tq, gw), lambda b, h, i: (b * nq + i, h))],
        out_specs=pl.BlockSpec((tq, gw), lambda b, h, i: (b * nq + i, h)),
        out_shape=jax.ShapeDtypeStruct((batch * seq, NSA_HEADS * HEAD_DIM), BF16),
        scratch_shapes=[pltpu.VMEM((nch, 1, cw), F32), pltpu.VMEM((nch, 1, cw), F32),
                        pltpu.VMEM((nch, HEAD_DIM, cw), F32)],
        compiler_params=_cparams("parallel", "parallel", "arbitrary"),
        name="nsa_sel_win",
    )(proj, proj, vst, proj, vwt, selmask, expand, gates, oc)


def _diff_attn_kernel(q_ref, k_ref, vt_ref, lamp_ref, subln_ref, o_ref, m_ref, l_ref, acc_ref, *, tk, lam_init):
    tq = q_ref.shape[0]
    s0 = pl.program_id(2) * tq
    q = q_ref[...]
    lane = lax.broadcasted_iota(jnp.int32, (1, LANES), 1)
    zero = jnp.zeros_like(q)
    qmaps = (jnp.where(lane < DIFF_QK, q, zero), jnp.where(lane >= DIFF_QK, q, zero))
    cw = acc_ref.shape[2]
    nj = tq // cw
    t = s0 + lax.broadcasted_iota(jnp.int32, (1, tq), 1)

    m_ref[...] = jnp.full(m_ref.shape, NEG_INF, F32)
    l_ref[...] = jnp.zeros(l_ref.shape, F32)
    acc_ref[...] = jnp.zeros(acc_ref.shape, F32)

    def step(kb, diag):
        k0 = pl.multiple_of(kb * tk, tk)
        kblk = k_ref[pl.ds(k0, tk), :]
        vt_blk = vt_ref[kb]
        todo = _causal_chains(diag, nj, cw, tk)
        chains = [(c, j, masked) for j, masked in todo for c in range(2)]
        scores = [_dot_nt(kblk, qmaps[c][j * cw:(j + 1) * cw]) for c, j, _ in chains]
        for (c, j, masked), st in zip(chains, scores):
            valid = None
            if masked:
                valid = k0 + lax.broadcasted_iota(jnp.int32, (tk, 1), 0) <= t[:, j * cw:(j + 1) * cw]
            _flash_update_t(st, vt_blk, m_ref, l_ref, acc_ref, c * nj + j, valid)

    n_full = s0 // tk

    def full_body(kb, c):
        step(kb, None)
        return c

    lax.fori_loop(0, n_full, full_body, 0)
    for d in range(tq // tk):
        step(n_full + d, d)

    lp = lamp_ref[...]
    lam = (jnp.exp(jnp.sum(lp[0:1] * lp[1:2], axis=-1, keepdims=True))
           - jnp.exp(jnp.sum(lp[2:3] * lp[3:4], axis=-1, keepdims=True)) + lam_init)
    for j in range(nj):
        a = acc_ref[j] / l_ref[j] - lam * (acc_ref[nj + j] / l_ref[nj + j])
        y = a * lax.rsqrt(jnp.mean(a * a, axis=0, keepdims=True) + EPS) * subln_ref[...]
        o_ref[j * cw:(j + 1) * cw, :] = (y * (1.0 - lam_init)).T.astype(o_ref.dtype)


def diff_attention(proj, vt, lam_params, subln, lam_init, batch, seq, tq, tk):
    nq = seq // tq
    cw = min(CHAIN_W, tq)
    return pl.pallas_call(
        functools.partial(_diff_attn_kernel, tk=tk, lam_init=lam_init),
        grid=(batch, DIFF_HEADS, nq),
        in_specs=[pl.BlockSpec((tq, HEAD_DIM), lambda b, h, i: (b * nq + i, CB_QB + h)),
                  pl.BlockSpec((seq, HEAD_DIM), lambda b, h, i: (b, CB_KB + h)),
                  pl.BlockSpec((None, None, seq // tk, HEAD_DIM, tk), lambda b, h, i: (b, h, 0, 0, 0)),
                  pl.BlockSpec((4, DIFF_QK), lambda b, h, i: (0, 0)),
                  pl.BlockSpec((None, HEAD_DIM, 1), lambda b, h, i: (h, 0, 0))],
        out_specs=pl.BlockSpec((tq, HEAD_DIM), lambda b, h, i: (b * nq + i, h)),
        out_shape=jax.ShapeDtypeStruct((batch * seq, DIFF_HEADS * HEAD_DIM), BF16),
        scratch_shapes=[pltpu.VMEM((2 * tq // cw, 1, cw), F32), pltpu.VMEM((2 * tq // cw, 1, cw), F32),
                        pltpu.VMEM((2 * tq // cw, HEAD_DIM, cw), F32)],
        compiler_params=_cparams("parallel", "parallel", "arbitrary"),
        name="diff_attention",
    )(proj, proj, vt, lam_params, subln)


def _lru_kernel(xb_ref, yb_ref, cw_ref, cb_ref, wa_ref, wx_ref, ba_ref, bx_ref, lam_ref, o_ref,
                xpad_ref, a_ref, u_ref, h_ref, carry_ref):
    tt = xb_ref.shape[0]
    pad = SUBLANES

    @pl.when(pl.program_id(2) == 0)
    def _():
        xpad_ref[0:pad, :] = jnp.zeros((pad, xpad_ref.shape[1]), F32)
        carry_ref[...] = jnp.zeros_like(carry_ref)

    xb = xb_ref[...]
    xpad_ref[pad:pad + tt, :] = xb
    cw = cw_ref[...]
    xc = cb_ref[...] + cw[CONV_W - 1:CONV_W] * xb
    for w in range(CONV_W - 1):
        off = pad - (CONV_W - 1) + w
        xc = xc + cw[w:w + 1] * xpad_ref[off:off + tt, :]
    xpad_ref[0:pad, :] = xb[tt - pad:tt]

    xcb = xc.astype(BF16)
    r = jax.nn.sigmoid(_dot(xcb, wa_ref[...]) + ba_ref[...])
    i = jax.nn.sigmoid(_dot(xcb, wx_ref[...]) + bx_ref[...])
    nl = -lam_ref[...]
    softplus = jnp.maximum(nl, 0.0) + jnp.log1p(jnp.exp(-jnp.abs(nl)))
    log_a = -LRU_C * r * softplus
    a = jnp.exp(log_a)
    a_ref[...] = a
    u_ref[...] = jnp.sqrt(1.0 - a * a) * (i * xc)

    row = lax.broadcasted_iota(jnp.int32, (SUBLANES, a.shape[1]), 0)

    def slab(j, hprev):
        r0 = pl.multiple_of(j * SUBLANES, SUBLANES)
        aa = a_ref[pl.ds(r0, SUBLANES), :]
        bb = u_ref[pl.ds(r0, SUBLANES), :]
        for d in (1, 2, 4):
            a_sh = jnp.where(row >= d, pltpu.roll(aa, shift=d, axis=0), 1.0)
            b_sh = jnp.where(row >= d, pltpu.roll(bb, shift=d, axis=0), 0.0)
            bb = aa * b_sh + bb
            aa = aa * a_sh
        hh = bb + aa * hprev
        h_ref[pl.ds(r0, SUBLANES), :] = hh
        return hh[SUBLANES - 1:SUBLANES, :]

    carry_ref[...] = lax.fori_loop(0, tt // SUBLANES, slab, carry_ref[...], unroll=4)
    o_ref[...] = (h_ref[...] * jax.nn.gelu(yb_ref[...])).astype(o_ref.dtype)


def lru(proj, conv_w, conv_b, wa, wx, ba, bx, lam, batch, seq, tt=512):
    w = proj.shape[1] // 2
    bw = w // LRU_BLOCKS
    nt = seq // tt
    vec = lambda: pl.BlockSpec((1, bw), lambda b, n, t: (0, n))
    return pl.pallas_call(
        _lru_kernel,
        grid=(batch, LRU_BLOCKS, nt),
        in_specs=[pl.BlockSpec((tt, bw), lambda b, n, t: (b * nt + t, n)),
                  pl.BlockSpec((tt, bw), lambda b, n, t: (b * nt + t, LRU_BLOCKS + n)),
                  pl.BlockSpec((CONV_W, bw), lambda b, n, t: (0, n)),
                  vec(),
                  pl.BlockSpec((None, bw, bw), lambda b, n, t: (n, 0, 0)),
                  pl.BlockSpec((None, bw, bw), lambda b, n, t: (n, 0, 0)),
                  vec(), vec(), vec()],
        out_specs=pl.BlockSpec((tt, bw), lambda b, n, t: (b * nt + t, n)),
        out_shape=jax.ShapeDtypeStruct((batch * seq, w), BF16),
        scratch_shapes=[pltpu.VMEM((tt + SUBLANES, bw), F32), pltpu.VMEM((tt, bw), F32),
                        pltpu.VMEM((tt, bw), F32), pltpu.VMEM((tt, bw), F32), pltpu.VMEM((1, bw), F32)],
        compiler_params=_cparams("parallel", "parallel", "arbitrary"),
        name="rglru",
    )(proj, proj, conv_w, conv_b, wa, wx, ba, bx, lam)


def _sel_constants(seq):
    n_blk = seq // CMP_STRIDE
    n_sel = seq // SEL_LEN
    c0 = np.arange(n_blk)[:, None] * CMP_STRIDE
    j0 = np.arange(LANES)[None, :] * SEL_LEN
    ov = np.clip(np.minimum(c0 + CMP_LEN, j0 + SEL_LEN) - np.maximum(c0, j0), 0, None) / CMP_LEN
    ov[n_blk - 1:, :] = 0.0
    ov[:, n_sel:] = 0.0
    expand = (np.arange(seq)[:, None] // SEL_LEN == np.arange(LANES)[None, :]).astype(np.float32)
    return jnp.asarray(ov.T, BF16), jnp.asarray(expand, BF16)


def _transposed_blocks(proj, col_block, n_heads, batch, seq, blk):
    v = proj[:, col_block * LANES:(col_block + n_heads) * LANES].reshape(batch, seq // blk, blk, n_heads, HEAD_DIM)
    return v.transpose(0, 3, 1, 4, 2)


def _attn_layer(h, layer_idx, g_mix, w_in, pe_k, w1_k, w2_k, pe_v, w1_v, w2_v,
                lq1, lk1, lq2, lk2, subln, w_out, batch, seq):
    a_q = NSA_HEADS * HEAD_DIM
    a_kv = NSA_KV * HEAD_DIM
    o2 = a_q + 6 * a_kv
    o3 = o2 + NSA_HEADS * N_BRANCH
    w_main = jnp.concatenate([w_in[:, :o2], w_in[:, o3:]], axis=1).astype(BF16)
    ng = NSA_GROUP * N_BRANCH
    w_gate = jnp.concatenate([jnp.pad(w_in[:, o2 + k * ng:o2 + (k + 1) * ng], ((0, 0), (0, LANES - ng)))
                              for k in range(NSA_KV)], axis=1).astype(BF16)
    scale = np.ones((1, N_MAIN_COLS), np.float32)
    scale[:, :a_q] = HEAD_DIM ** -0.5 * LOG2E
    scale[:, CB_QB * LANES:CB_KB * LANES] = DIFF_QK ** -0.5 * LOG2E
    g = g_mix.reshape(1, -1)
    proj, gates = norm_matmul(h, g, w_main, jnp.asarray(scale), BF16, w_side=w_gate)

    nb = seq // CMP_STRIDE
    x4 = proj[:, CB_KCMP * LANES:CB_KSEL * LANES].reshape(batch, nb, CMP_STRIDE, 2 * NSA_KV, HEAD_DIM)
    x4 = x4.transpose(0, 3, 1, 2, 4).reshape(batch, 2 * NSA_KV, nb, CMP_STRIDE * HEAD_DIM)
    flat = lambda a: a.reshape(1, CMP_STRIDE * HEAD_DIM)
    pelo = jnp.stack([flat(pe_k[:CMP_STRIDE]), flat(pe_v[:CMP_STRIDE])])
    pehi = jnp.stack([flat(pe_k[CMP_STRIDE:]), flat(pe_v[CMP_STRIDE:])])
    wflat = lambda a: a.reshape(CMP_STRIDE * HEAD_DIM, HEAD_DIM)
    w1lo = jnp.stack([wflat(w1_k[:CMP_STRIDE]), wflat(w1_v[:CMP_STRIDE])]).astype(BF16)
    w1hi = jnp.stack([wflat(w1_k[CMP_STRIDE:]), wflat(w1_v[CMP_STRIDE:])]).astype(BF16)
    w2 = jnp.stack([w2_k, w2_v]).astype(BF16)
    kvc = compress(x4, pelo, pehi, w1lo, w1hi, w2)

    selmap_t, expand = _sel_constants(seq)
    vct = kvc[:, NSA_KV:].transpose(0, 1, 3, 2)
    oc, selmask = nsa_cmp_select(proj, kvc, vct, gates, selmap_t, batch, seq)
    tq, tk = min(NSA_TQ, seq), min(NSA_TK, seq)
    vst = _transposed_blocks(proj, CB_VSEL, NSA_KV, batch, seq, tk)
    vwt = _transposed_blocks(proj, CB_VWIN, NSA_KV, batch, seq, min(CHAIN_W, tq))
    o_a = nsa_sel_win(proj, vst, vwt, selmask, expand, gates, oc, batch, seq, tq, tk)

    lam_init = 0.8 - 0.6 * math.exp(-0.3 * layer_idx)
    lam_params = jnp.stack([lq1, lk1, lq2, lk2]).astype(F32)
    tk = min(DIFF_TK, seq)
    vt = _transposed_blocks(proj, CB_VB, DIFF_HEADS, batch, seq, tk)
    o_b = diff_attention(proj, vt, lam_params, subln.reshape(DIFF_HEADS, HEAD_DIM, 1), lam_init, batch, seq,
                         min(DIFF_TQ, seq), tk)

    w_out = w_out.astype(BF16)
    return matmul_residual([o_a, o_b], [w_out[:a_q], w_out[a_q:]], h)


def _rec_layer(h, g_mix, w_in, conv_w, conv_b, wa, ba, wx, bx, lam_p, w_out, batch, seq):
    n = w_in.shape[1]
    proj = norm_matmul(h, g_mix.reshape(1, -1), w_in.astype(BF16), jnp.ones((1, n), F32), F32)
    row = lambda v: v.reshape(1, -1)
    hy = lru(proj, conv_w, row(conv_b), wa.astype(BF16), wx.astype(BF16), row(ba), row(bx), row(lam_p), batch, seq)
    return matmul_residual([hy], [w_out.astype(BF16)], h)


def kernel(x, p, g_mix, g_ffn, g_ple, g_final, w_in_attn, cmp_pe_k, cmp_w1_k, cmp_w2_k, cmp_pe_v, cmp_w1_v, cmp_w2_v, diff_lq1, diff_lk1, diff_lq2, diff_lk2, diff_subln, w_out_attn, w_in_rec, conv_w, conv_b, lru_wa, lru_ba, lru_wx, lru_bx, lru_lambda, w_out_rec, w_ffn_gate, w_ffn_up, w_ffn_down, w_ple_proj, w_ple_gate):
    batch, seq, d = x.shape
    depth = p.shape[0]
    h = x.reshape(batch * seq, d)
    for i in range(depth):
        j = i // 2
        if i % 2 == 0:
            h = _attn_layer(h, i, g_mix[i], w_in_attn[j], cmp_pe_k[j], cmp_w1_k[j], cmp_w2_k[j],
                            cmp_pe_v[j], cmp_w1_v[j], cmp_w2_v[j], diff_lq1[j], diff_lk1[j],
                            diff_lq2[j], diff_lk2[j], diff_subln[j], w_out_attn[j], batch, seq)
        else:
            h = _rec_layer(h, g_mix[i], w_in_rec[j], conv_w[j], conv_b[j], lru_wa[j], lru_ba[j],
                           lru_wx[j], lru_bx[j], lru_lambda[j], w_out_rec[j], batch, seq)
        h = ffn(h, g_ffn[i].reshape(1, -1), w_ffn_gate[i].astype(BF16), w_ffn_up[i].astype(BF16),
                w_ffn_down[i].astype(BF16))
        h = ple(h, g_ple[i].reshape(1, -1), p[i].reshape(batch * seq, -1),
                w_ple_gate[i].astype(BF16), w_ple_proj[i].astype(BF16))
    return final_norm(h, g_final.reshape(1, -1)).reshape(batch, seq, d)
```

```python
import functools
import math

import numpy as np
import jax
import jax.numpy as jnp
from jax import lax
from jax.experimental import pallas as pl
from jax.experimental.pallas import tpu as pltpu

F32 = jnp.float32
BF16 = jnp.bfloat16

EPS = 1e-6
NEG_INF = -1e30
BIG = 1e6

NSA_HEADS = 8
NSA_KV = 2
NSA_GROUP = NSA_HEADS // NSA_KV
HEAD_DIM = 128
CMP_STRIDE = 16
CMP_LEN = 2 * CMP_STRIDE
SEL_LEN = 64
SEL_TOPK = 16
WIN = 512
N_BRANCH = 3
DIFF_HEADS = 8
DIFF_QK = 64
CONV_W = 4
LRU_BLOCKS = 8
LRU_C = 8.0

LANES = 128
SUBLANES = 8
VMEM_LIMIT = 56 * 1024 * 1024

CB_QA = 0
CB_KCMP, CB_VCMP = 8, 10
CB_KSEL, CB_VSEL = 12, 14
CB_KWIN, CB_VWIN = 16, 18
CB_QB, CB_KB, CB_VB = 20, 28, 36
N_MAIN_COLS = 44 * LANES

LOG2E = math.log2(math.e)
DIFF_TQ, DIFF_TK = 2048, 512
CHAIN_W = 256
NSA_TQ, NSA_TK = 1024, 512


def _cparams(*sem):
    return pltpu.CompilerParams(dimension_semantics=sem, vmem_limit_bytes=VMEM_LIMIT)


def _dot(a, b):
    return jnp.dot(a, b, preferred_element_type=F32)


def _dot_nt(a, b):
    return lax.dot_general(a, b, (((1,), (1,)), ((), ())), preferred_element_type=F32)


def _rms(x, g):
    return x * lax.rsqrt(jnp.mean(x * x, axis=-1, keepdims=True) + EPS) * g


def _norm_matmul_kernel(x_ref, g_ref, w_ref, s_ref, *rest):
    if len(rest) == 4:
        w_side_ref, o_ref, o_side_ref, xn_ref = rest
    else:
        (o_ref, xn_ref), w_side_ref, o_side_ref = rest, None, None

    @pl.when(pl.program_id(1) == 0)
    def _():
        xn_ref[...] = _rms(x_ref[...], g_ref[...]).astype(BF16)
        if w_side_ref is not None:
            o_side_ref[...] = _dot(xn_ref[...], w_side_ref[...])

    o_ref[...] = (_dot(xn_ref[...], w_ref[...]) * s_ref[...]).astype(o_ref.dtype)


def norm_matmul(x, g, w, col_scale, out_dtype, w_side=None, tm=1024, tn=512):
    m, d = x.shape
    n = w.shape[1]
    tm, tn = min(tm, m), min(tn, n)
    in_specs = [pl.BlockSpec((tm, d), lambda i, j: (i, 0)),
                pl.BlockSpec((1, d), lambda i, j: (0, 0)),
                pl.BlockSpec((d, tn), lambda i, j: (0, j)),
                pl.BlockSpec((1, tn), lambda i, j: (0, j))]
    out_specs = [pl.BlockSpec((tm, tn), lambda i, j: (i, j))]
    out_shape = [jax.ShapeDtypeStruct((m, n), out_dtype)]
    args = [x, g, w, col_scale]
    if w_side is not None:
        n2 = w_side.shape[1]
        in_specs.append(pl.BlockSpec((d, n2), lambda i, j: (0, 0)))
        out_specs.append(pl.BlockSpec((tm, n2), lambda i, j: (i, 0)))
        out_shape.append(jax.ShapeDtypeStruct((m, n2), F32))
        args.append(w_side)
    out = pl.pallas_call(
        _norm_matmul_kernel,
        grid=(m // tm, n // tn),
        in_specs=in_specs,
        out_specs=out_specs,
        out_shape=out_shape,
        scratch_shapes=[pltpu.VMEM((tm, d), BF16)],
        compiler_params=_cparams("parallel", "arbitrary"),
        name="norm_matmul",
    )(*args)
    return out if w_side is not None else out[0]


def _matmul_res_kernel(*refs, tn):
    n_ops = (len(refs) - 2) // 2
    res_ref, o_ref = refs[2 * n_ops], refs[2 * n_ops + 1]
    for c in range(o_ref.shape[1] // tn):
        cs = slice(c * tn, (c + 1) * tn)
        acc = res_ref[:, cs]
        for a_ref, w_ref in zip(refs[:n_ops], refs[n_ops:2 * n_ops]):
            acc = acc + _dot(a_ref[...], w_ref[:, cs])
        o_ref[:, cs] = acc


def matmul_residual(a_list, w_list, res, tm=512, tn=512):
    m, n = res.shape
    tm = min(tm, m)
    in_specs = [pl.BlockSpec((tm, a.shape[1]), lambda i: (i, 0)) for a in a_list]
    in_specs += [pl.BlockSpec(w.shape, lambda i: (0, 0)) for w in w_list]
    in_specs += [pl.BlockSpec((tm, n), lambda i: (i, 0))]
    return pl.pallas_call(
        functools.partial(_matmul_res_kernel, tn=tn),
        grid=(m // tm,),
        in_specs=in_specs,
        out_specs=pl.BlockSpec((tm, n), lambda i: (i, 0)),
        out_shape=jax.ShapeDtypeStruct((m, n), F32),
        compiler_params=_cparams("parallel"),
        name="matmul_residual",
    )(*a_list, *w_list, res)


def _ffn_kernel(x_ref, g_ref, wg_ref, wu_ref, wd_ref, o_ref, xn_ref, acc_ref):
    f = pl.program_id(1)

    @pl.when(f == 0)
    def _():
        xn_ref[...] = _rms(x_ref[...], g_ref[...]).astype(BF16)
        acc_ref[...] = jnp.zeros_like(acc_ref)

    xn = xn_ref[...]
    act = jax.nn.silu(_dot(xn, wg_ref[...])) * _dot(xn, wu_ref[...])
    acc_ref[...] += _dot(act.astype(BF16), wd_ref[...])

    @pl.when(f == pl.num_programs(1) - 1)
    def _():
        o_ref[...] = x_ref[...] + acc_ref[...]


def ffn(x, g, wg, wu, wd, tm=512, tf=512):
    m, d = x.shape
    dff = wg.shape[1]
    return pl.pallas_call(
        _ffn_kernel,
        grid=(m // tm, dff // tf),
        in_specs=[pl.BlockSpec((tm, d), lambda i, f: (i, 0)),
                  pl.BlockSpec((1, d), lambda i, f: (0, 0)),
                  pl.BlockSpec((d, tf), lambda i, f: (0, f)),
                  pl.BlockSpec((d, tf), lambda i, f: (0, f)),
                  pl.BlockSpec((tf, d), lambda i, f: (f, 0))],
        out_specs=pl.BlockSpec((tm, d), lambda i, f: (i, 0)),
        out_shape=jax.ShapeDtypeStruct((m, d), F32),
        scratch_shapes=[pltpu.VMEM((tm, d), BF16), pltpu.VMEM((tm, d), F32)],
        compiler_params=_cparams("parallel", "arbitrary"),
        name="ffn",
    )(x, g, wg, wu, wd)


def _ple_kernel(x_ref, g_ref, p_ref, wg_ref, wp_ref, *rest, tn):
    o_ref = rest[-1]
    xn = _rms(x_ref[...], g_ref[...]).astype(BF16)
    pb = p_ref[...].astype(BF16)
    for c in range(o_ref.shape[1] // tn):
        cs = slice(c * tn, (c + 1) * tn)
        gate = jax.nn.sigmoid(_dot(xn, wg_ref[:, cs]))
        o_ref[:, cs] = x_ref[:, cs] + gate * _dot(pb, wp_ref[:, cs])
    if len(rest) == 2:
        o_ref[...] = _rms(o_ref[...], rest[0][...])


def ple(x, g, p, wgate, wproj, g_out=None, tm=512, tn=512):
    m, d = x.shape
    tm = min(tm, m)
    pd = p.shape[1]
    in_specs = [pl.BlockSpec((tm, d), lambda i: (i, 0)),
                pl.BlockSpec((1, d), lambda i: (0, 0)),
                pl.BlockSpec((tm, pd), lambda i: (i, 0)),
                pl.BlockSpec((d, d), lambda i: (0, 0)),
                pl.BlockSpec((pd, d), lambda i: (0, 0))]
    args = [x, g, p, wgate, wproj]
    if g_out is not None:
        in_specs.append(pl.BlockSpec((1, d), lambda i: (0, 0)))
        args.append(g_out)
    return pl.pallas_call(
        functools.partial(_ple_kernel, tn=tn),
        grid=(m // tm,),
        in_specs=in_specs,
        out_specs=pl.BlockSpec((tm, d), lambda i: (i, 0)),
        out_shape=jax.ShapeDtypeStruct((m, d), F32),
        compiler_params=_cparams("parallel"),
        name="ple",
    )(*args)


def _compress_kernel(x_ref, pelo_ref, pehi_ref, w1lo_ref, w1hi_ref, w2_ref, o_ref):
    x = x_ref[...].astype(F32)
    nb = x.shape[0]
    lo = _dot((x + pelo_ref[...]).astype(BF16), w1lo_ref[...])
    hi = _dot((x + pehi_ref[...]).astype(BF16), w1hi_ref[...])
    z = lo + pltpu.roll(hi, shift=nb - 1, axis=0)
    y = _dot(jax.nn.gelu(z).astype(BF16), w2_ref[...])
    row = lax.broadcasted_iota(jnp.int32, y.shape, 0)
    o_ref[...] = jnp.where(row < nb - 1, y, 0.0).astype(o_ref.dtype)


def compress(x4, pelo, pehi, w1lo, w1hi, w2):
    b, _, nb, kd = x4.shape
    sel = lambda bi, c: (c // NSA_KV, 0, 0)
    return pl.pallas_call(
        _compress_kernel,
        grid=(b, 2 * NSA_KV),
        in_specs=[pl.BlockSpec((None, None, nb, kd), lambda bi, c: (bi, c, 0, 0)),
                  pl.BlockSpec((None, 1, kd), sel),
                  pl.BlockSpec((None, 1, kd), sel),
                  pl.BlockSpec((None, kd, HEAD_DIM), sel),
                  pl.BlockSpec((None, kd, HEAD_DIM), sel),
                  pl.BlockSpec((None, HEAD_DIM, HEAD_DIM), sel)],
        out_specs=pl.BlockSpec((None, None, nb, HEAD_DIM), lambda bi, c: (bi, c, 0, 0)),
        out_shape=jax.ShapeDtypeStruct((b, 2 * NSA_KV, nb, HEAD_DIM), BF16),
        compiler_params=_cparams("parallel", "parallel"),
        name="nsa_compress",
    )(x4, pelo, pehi, w1lo, w1hi, w2)


def _stack_heads(q):
    return jnp.concatenate([q[:, g * HEAD_DIM:(g + 1) * HEAD_DIM] for g in range(NSA_GROUP)], axis=0)


def _nsa_cmp_kernel(q_ref, kc_ref, vct_ref, gate_ref, selmap_t_ref, oc_ref, sel_t_ref, *, n_cmp, n_sel, topk):
    tq = q_ref.shape[0]
    s0 = pl.program_id(2) * tq
    q4 = _stack_heads(q_ref[...])
    st = _dot_nt(kc_ref[...], q4)
    t1 = s0 + lax.broadcasted_iota(jnp.int32, (1, tq), 1)
    t = jnp.concatenate([t1] * NSA_GROUP, axis=1)
    n = lax.broadcasted_iota(jnp.int32, (st.shape[0], 1), 0)
    valid = (n * CMP_STRIDE + (CMP_LEN - 1) <= t) & (n < n_cmp)
    sm = jnp.where(valid, st, NEG_INF)
    e = jnp.where(valid, jnp.exp2(sm - jnp.max(sm, axis=0, keepdims=True)), 0.0)
    l = jnp.sum(e, axis=0, keepdims=True)
    p = e * (1.0 / jnp.where(l > 0.0, l, 1.0))
    o_t = _dot(vct_ref[...], p.astype(BF16))

    gates_t = jax.nn.sigmoid(gate_ref[...]).T
    for g in range(NSA_GROUP):
        r = g * N_BRANCH
        oc_ref[:, g * HEAD_DIM:(g + 1) * HEAD_DIM] = (gates_t[r:r + 1] * o_t[:, g * tq:(g + 1) * tq]).T

    ps = p[:, 0:tq]
    for g in range(1, NSA_GROUP):
        ps = ps + p[:, g * tq:(g + 1) * tq]
    ps_hi = ps.astype(BF16)
    ps_lo = (ps - ps_hi.astype(F32)).astype(BF16)
    imp = _dot(selmap_t_ref[...], ps_hi) + _dot(selmap_t_ref[...], ps_lo)
    blk = lax.broadcasted_iota(jnp.int32, (LANES, 1), 0)
    cur = t1 // SEL_LEN
    forced = (blk == 0) | (blk == cur) | (blk == cur - 1)
    imp = jnp.where(blk > cur, -BIG, jnp.where(forced, BIG, imp))
    imp = jnp.where(blk < n_sel, imp, -jnp.inf)
    sel = jnp.zeros(imp.shape, F32)
    blk_f = blk.astype(F32)
    for _ in range(topk):
        mx = jnp.max(imp, axis=0, keepdims=True)
        idx = jnp.min(jnp.where(imp == mx, blk_f, float(LANES)), axis=0, keepdims=True)
        pick = blk_f == idx
        sel = jnp.where(pick, 1.0, sel)
        imp = jnp.where(pick, -jnp.inf, imp)
    sel_t_ref[...] = sel.astype(sel_t_ref.dtype)


def nsa_cmp_select(proj, kvc, vct, gates, selmap_t, batch, seq, tq=256):
    nb = kvc.shape[2]
    tq = min(tq, seq)
    nq = seq // tq
    n_sel = seq // SEL_LEN
    kern = functools.partial(_nsa_cmp_kernel, n_cmp=nb - 1, n_sel=n_sel, topk=min(SEL_TOPK, n_sel))
    gw = NSA_GROUP * HEAD_DIM
    return pl.pallas_call(
        kern,
        grid=(batch, NSA_KV, nq),
        in_specs=[pl.BlockSpec((tq, gw), lambda b, h, i: (b * nq + i, h)),
                  pl.BlockSpec((None, None, nb, HEAD_DIM), lambda b, h, i: (b, h, 0, 0)),
                  pl.BlockSpec((None, None, HEAD_DIM, nb), lambda b, h, i: (b, h, 0, 0)),
                  pl.BlockSpec((tq, LANES), lambda b, h, i: (b * nq + i, h)),
                  pl.BlockSpec((LANES, nb), lambda b, h, i: (0, 0))],
        out_specs=[pl.BlockSpec((tq, gw), lambda b, h, i: (b * nq + i, h)),
                   pl.BlockSpec((None, None, LANES, tq), lambda b, h, i: (b, h, 0, i))],
        out_shape=[jax.ShapeDtypeStruct((batch * seq, NSA_HEADS * HEAD_DIM), F32),
                   jax.ShapeDtypeStruct((batch, NSA_KV, LANES, seq), BF16)],
        compiler_params=_cparams("parallel", "parallel", "arbitrary"),
        name="nsa_cmp_select",
    )(proj, kvc, vct, gates, selmap_t)


def _flash_update_t(st, vt_blk, m_ref, l_ref, acc_ref, c, valid):
    if valid is not None:
        st = jnp.where(valid, st, NEG_INF)
    m_prev = m_ref[c]
    m_new = jnp.maximum(m_prev, jnp.max(st, axis=0, keepdims=True))
    p = jnp.exp2(st - m_new)
    alpha = jnp.exp2(m_prev - m_new)
    l_ref[c] = alpha * l_ref[c] + jnp.sum(p, axis=0, keepdims=True)
    acc_ref[c] = alpha * acc_ref[c] + _dot(vt_blk, p.astype(BF16))
    m_ref[c] = m_new


def _causal_chains(diag, nj, cw, tk):
    if diag is None:
        return [(j, False) for j in range(nj)]
    lo, hi = diag * tk, (diag + 1) * tk - 1
    return [(j, hi > j * cw) for j in range(nj) if lo <= (j + 1) * cw - 1]


def _nsa_sel_win_kernel(q_ref, ks_ref, vst_ref, kw_ref, vwt_ref, sel_ref, expand_ref, gate_ref, oc_ref,
                        o_ref, m_ref, l_ref, acc_ref, *, tk):
    tq = q_ref.shape[0]
    cw = acc_ref.shape[2]
    nj = tq // cw
    s0 = pl.program_id(2) * tq
    q = q_ref[...]
    qc = [[q[j * cw:(j + 1) * cw, g * HEAD_DIM:(g + 1) * HEAD_DIM] for j in range(nj)] for g in range(NSA_GROUP)]
    t = s0 + lax.broadcasted_iota(jnp.int32, (1, tq), 1)
    sel_t = sel_ref[...]

    m_ref[...] = jnp.full(m_ref.shape, NEG_INF, F32)
    l_ref[...] = jnp.zeros(l_ref.shape, F32)
    acc_ref[...] = jnp.zeros(acc_ref.shape, F32)

    def step(kb, diag):
        k0 = pl.multiple_of(kb * tk, tk)
        kblk = ks_ref[pl.ds(k0, tk), :]
        vt_blk = vst_ref[kb]
        eblk = expand_ref[pl.ds(k0, tk), :]
        todo = _causal_chains(diag, nj, cw, tk)
        chosen = {j: _dot(eblk, sel_t[:, j * cw:(j + 1) * cw]) > 0.5 for j, _ in todo}
        scores = {(g, j): _dot_nt(kblk, qc[g][j]) for j, _ in todo for g in range(NSA_GROUP)}
        for j, masked in todo:
            valid = chosen[j]
            if masked:
                valid = valid & (k0 + lax.broadcasted_iota(jnp.int32, (tk, 1), 0) <= t[:, j * cw:(j + 1) * cw])
            for g in range(NSA_GROUP):
                _flash_update_t(scores[(g, j)], vt_blk, m_ref, l_ref, acc_ref, g * nj + j, valid)

    n_full = s0 // tk

    def full_body(kb, c):
        step(kb, None)
        return c

    lax.fori_loop(0, n_full, full_body, 0)
    for d in range(tq // tk):
        step(n_full + d, d)

    nwb = WIN // cw + 1
    gates_t = jax.nn.sigmoid(gate_ref[...]).T
    for j in range(nj):
        j0 = jnp.maximum(s0 // cw + j - WIN // cw, 0)
        w0 = pl.multiple_of(j0 * cw, cw)
        kwblk = kw_ref[pl.ds(w0, nwb * cw), :]
        tj = t[:, j * cw:(j + 1) * cw]
        dlt = tj - (w0 + lax.broadcasted_iota(jnp.int32, (nwb * cw, 1), 0))
        inwin = (dlt >= 0) & (dlt < WIN)
        wscores = [_dot_nt(kwblk, qc[g][j]) for g in range(NSA_GROUP)]
        for g in range(NSA_GROUP):
            sw = jnp.where(inwin, wscores[g], NEG_INF)
            pw = jnp.exp2(sw - jnp.max(sw, axis=0, keepdims=True))
            lw = jnp.sum(pw, axis=0, keepdims=True)
            pw = pw.astype(BF16)
            o_win = _dot(vwt_ref[j0], pw[0:cw])
            for jj in range(1, nwb):
                o_win = o_win + _dot(vwt_ref[j0 + jj], pw[jj * cw:(jj + 1) * cw])
            r = g * N_BRANCH
            c = g * nj + j
            g_sel = gates_t[r + 1:r + 2, j * cw:(j + 1) * cw]
            g_win = gates_t[r + 2:r + 3, j * cw:(j + 1) * cw]
            o_t = g_sel * (acc_ref[c] / l_ref[c]) + g_win * (o_win / lw)
            rs, cs = slice(j * cw, (j + 1) * cw), slice(g * HEAD_DIM, (g + 1) * HEAD_DIM)
            o_ref[rs, cs] = (oc_ref[rs, cs] + o_t.T).astype(o_ref.dtype)


def nsa_sel_win(proj, vst, vwt, selmask, expand, gates, oc, batch, seq, tq, tk):
    nq = seq // tq
    cw = vwt.shape[-1]
    nch = NSA_GROUP * tq // cw
    gw = NSA_GROUP * HEAD_DIM
    kv_spec = lambda cb: pl.BlockSpec((seq, HEAD_DIM), lambda b, h, i: (b, cb + h))
    vt_spec = lambda blk: pl.BlockSpec((None, None, seq // blk, HEAD_DIM, blk), lambda b, h, i: (b, h, 0, 0, 0))
    return pl.pallas_call(
        functools.partial(_nsa_sel_win_kernel, tk=tk),
        grid=(batch, NSA_KV, nq),
        in_specs=[pl.BlockSpec((tq, gw), lambda b, h, i: (b * nq + i, h)),
                  kv_spec(CB_KSEL), vt_spec(tk), kv_spec(CB_KWIN), vt_spec(cw),
                  pl.BlockSpec((None, None, LANES, tq), lambda b, h, i: (b, h, 0, i)),
                  pl.BlockSpec((seq, LANES), lambda b, h, i: (0, 0)),
                  pl.BlockSpec((tq, LANES), lambda b, h, i: (b * nq + i, h)),
                  pl.BlockSpec((tq, gw), lambda b, h, i: (b * nq + i, h))],
        out_specs=pl.BlockSpec((tq, gw), lambda b, h, i: (b * nq + i, h)),
        out_shape=jax.ShapeDtypeStruct((batch * seq, NSA_HEADS * HEAD_DIM), BF16),
        scratch_shapes=[pltpu.VMEM((nch, 1, cw), F32), pltpu.VMEM((nch, 1, cw), F32),
                        pltpu.VMEM((nch, HEAD_DIM, cw), F32)],
        compiler_params=_cparams("parallel", "parallel", "arbitrary"),
        name="nsa_sel_win",
    )(proj, proj, vst, proj, vwt, selmask, expand, gates, oc)


def _diff_attn_kernel(q_ref, k_ref, vt_ref, lamp_ref, subln_ref, o_ref, m_ref, l_ref, acc_ref, *, tk, lam_init):
    tq = q_ref.shape[0]
    s0 = pl.program_id(2) * tq
    q = q_ref[...]
    lane = lax.broadcasted_iota(jnp.int32, (1, LANES), 1)
    zero = jnp.zeros_like(q)
    qmaps = (jnp.where(lane < DIFF_QK, q, zero), jnp.where(lane >= DIFF_QK, q, zero))
    cw = acc_ref.shape[2]
    nj = tq // cw
    t = s0 + lax.broadcasted_iota(jnp.int32, (1, tq), 1)

    m_ref[...] = jnp.full(m_ref.shape, NEG_INF, F32)
    l_ref[...] = jnp.zeros(l_ref.shape, F32)
    acc_ref[...] = jnp.zeros(acc_ref.shape, F32)

    def step(kb, diag):
        k0 = pl.multiple_of(kb * tk, tk)
        kblk = k_ref[pl.ds(k0, tk), :]
        vt_blk = vt_ref[kb]
        todo = _causal_chains(diag, nj, cw, tk)
        chains = [(c, j, masked) for j, masked in todo for c in range(2)]
        scores = [_dot_nt(kblk, qmaps[c][j * cw:(j + 1) * cw]) for c, j, _ in chains]
        for (c, j, masked), st in zip(chains, scores):
            valid = None
            if masked:
                valid = k0 + lax.broadcasted_iota(jnp.int32, (tk, 1), 0) <= t[:, j * cw:(j + 1) * cw]
            _flash_update_t(st, vt_blk, m_ref, l_ref, acc_ref, c * nj + j, valid)

    n_full = s0 // tk

    def full_body(kb, c):
        step(kb, None)
        return c

    lax.fori_loop(0, n_full, full_body, 0)
    for d in range(tq // tk):
        step(n_full + d, d)

    lp = lamp_ref[...]
    lam = (jnp.exp(jnp.sum(lp[0:1] * lp[1:2], axis=-1, keepdims=True))
           - jnp.exp(jnp.sum(lp[2:3] * lp[3:4], axis=-1, keepdims=True)) + lam_init)
    for j in range(nj):
        a = acc_ref[j] / l_ref[j] - lam * (acc_ref[nj + j] / l_ref[nj + j])
        y = a * lax.rsqrt(jnp.mean(a * a, axis=0, keepdims=True) + EPS) * subln_ref[...]
        o_ref[j * cw:(j + 1) * cw, :] = (y * (1.0 - lam_init)).T.astype(o_ref.dtype)


def diff_attention(proj, vt, lam_params, subln, lam_init, batch, seq, tq, tk):
    nq = seq // tq
    cw = min(CHAIN_W, tq)
    return pl.pallas_call(
        functools.partial(_diff_attn_kernel, tk=tk, lam_init=lam_init),
        grid=(batch, DIFF_HEADS, nq),
        in_specs=[pl.BlockSpec((tq, HEAD_DIM), lambda b, h, i: (b * nq + i, CB_QB + h)),
                  pl.BlockSpec((seq, HEAD_DIM), lambda b, h, i: (b, CB_KB + h)),
                  pl.BlockSpec((None, None, seq // tk, HEAD_DIM, tk), lambda b, h, i: (b, h, 0, 0, 0)),
                  pl.BlockSpec((4, DIFF_QK), lambda b, h, i: (0, 0)),
                  pl.BlockSpec((None, HEAD_DIM, 1), lambda b, h, i: (h, 0, 0))],
        out_specs=pl.BlockSpec((tq, HEAD_DIM), lambda b, h, i: (b * nq + i, h)),
        out_shape=jax.ShapeDtypeStruct((batch * seq, DIFF_HEADS * HEAD_DIM), BF16),
        scratch_shapes=[pltpu.VMEM((2 * tq // cw, 1, cw), F32), pltpu.VMEM((2 * tq // cw, 1, cw), F32),
                        pltpu.VMEM((2 * tq // cw, HEAD_DIM, cw), F32)],
        compiler_params=_cparams("parallel", "parallel", "arbitrary"),
        name="diff_attention",
    )(proj, proj, vt, lam_params, subln)


def _lru_kernel(xb_ref, yb_ref, cw_ref, cb_ref, wa_ref, wx_ref, ba_ref, bx_ref, lam_ref, o_ref,
                xpad_ref, a_ref, u_ref, h_ref, carry_ref):
    tt = xb_ref.shape[0]
    pad = SUBLANES

    @pl.when(pl.program_id(2) == 0)
    def _():
        xpad_ref[0:pad, :] = jnp.zeros((pad, xpad_ref.shape[1]), F32)
        carry_ref[...] = jnp.zeros_like(carry_ref)

    xb = xb_ref[...]
    xpad_ref[pad:pad + tt, :] = xb
    cw = cw_ref[...]
    xc = cb_ref[...] + cw[CONV_W - 1:CONV_W] * xb
    for w in range(CONV_W - 1):
        off = pad - (CONV_W - 1) + w
        xc = xc + cw[w:w + 1] * xpad_ref[off:off + tt, :]
    xpad_ref[0:pad, :] = xb[tt - pad:tt]

    xcb = xc.astype(BF16)
    r = jax.nn.sigmoid(_dot(xcb, wa_ref[...]) + ba_ref[...])
    i = jax.nn.sigmoid(_dot(xcb, wx_ref[...]) + bx_ref[...])
    nl = -lam_ref[...]
    softplus = jnp.maximum(nl, 0.0) + jnp.log1p(jnp.exp(-jnp.abs(nl)))
    log_a = -LRU_C * r * softplus
    a = jnp.exp(log_a)
    a_ref[...] = a
    u_ref[...] = jnp.sqrt(1.0 - a * a) * (i * xc)

    row = lax.broadcasted_iota(jnp.int32, (SUBLANES, a.shape[1]), 0)

    def slab(j, hprev):
        r0 = pl.multiple_of(j * SUBLANES, SUBLANES)
        aa = a_ref[pl.ds(r0, SUBLANES), :]
        bb = u_ref[pl.ds(r0, SUBLANES), :]
        for d in (1, 2, 4):
            a_sh = jnp.where(row >= d, pltpu.roll(aa, shift=d, axis=0), 1.0)
            b_sh = jnp.where(row >= d, pltpu.roll(bb, shift=d, axis=0), 0.0)
            bb = aa * b_sh + bb
            aa = aa * a_sh
        hh = bb + aa * hprev
        h_ref[pl.ds(r0, SUBLANES), :] = hh
        return hh[SUBLANES - 1:SUBLANES, :]

    carry_ref[...] = lax.fori_loop(0, tt // SUBLANES, slab, carry_ref[...], unroll=4)
    o_ref[...] = (h_ref[...] * jax.nn.gelu(yb_ref[...])).astype(o_ref.dtype)


def lru(proj, conv_w, conv_b, wa, wx, ba, bx, lam, batch, seq, tt=512):
    w = proj.shape[1] // 2
    bw = w // LRU_BLOCKS
    nt = seq // tt
    vec = lambda: pl.BlockSpec((1, bw), lambda b, n, t: (0, n))
    return pl.pallas_call(
        _lru_kernel,
        grid=(batch, LRU_BLOCKS, nt),
        in_specs=[pl.BlockSpec((tt, bw), lambda b, n, t: (b * nt + t, n)),
                  pl.BlockSpec((tt, bw), lambda b, n, t: (b * nt + t, LRU_BLOCKS + n)),
                  pl.BlockSpec((CONV_W, bw), lambda b, n, t: (0, n)),
                  vec(),
                  pl.BlockSpec((None, bw, bw), lambda b, n, t: (n, 0, 0)),
                  pl.BlockSpec((None, bw, bw), lambda b, n, t: (n, 0, 0)),
                  vec(), vec(), vec()],
        out_specs=pl.BlockSpec((tt, bw), lambda b, n, t: (b * nt + t, n)),
        out_shape=jax.ShapeDtypeStruct((batch * seq, w), BF16),
        scratch_shapes=[pltpu.VMEM((tt + SUBLANES, bw), F32), pltpu.VMEM((tt, bw), F32),
                        pltpu.VMEM((tt, bw), F32), pltpu.VMEM((tt, bw), F32), pltpu.VMEM((1, bw), F32)],
        compiler_params=_cparams("parallel", "parallel", "arbitrary"),
        name="rglru",
    )(proj, proj, conv_w, conv_b, wa, wx, ba, bx, lam)


def _sel_constants(seq):
    n_blk = seq // CMP_STRIDE
    n_sel = seq // SEL_LEN
    c0 = np.arange(n_blk)[:, None] * CMP_STRIDE
    j0 = np.arange(LANES)[None, :] * SEL_LEN
    ov = np.clip(np.minimum(c0 + CMP_LEN, j0 + SEL_LEN) - np.maximum(c0, j0), 0, None) / CMP_LEN
    ov[n_blk - 1:, :] = 0.0
    ov[:, n_sel:] = 0.0
    expand = (np.arange(seq)[:, None] // SEL_LEN == np.arange(LANES)[None, :]).astype(np.float32)
    return jnp.asarray(ov.T, BF16), jnp.asarray(expand, BF16)


def _transposed_blocks(proj, col_block, n_heads, batch, seq, blk):
    v = proj[:, col_block * LANES:(col_block + n_heads) * LANES].reshape(batch, seq // blk, blk, n_heads, HEAD_DIM)
    return v.transpose(0, 3, 1, 4, 2)


def _attn_layer(h, layer_idx, g_mix, w_in, pe_k, w1_k, w2_k, pe_v, w1_v, w2_v,
                lq1, lk1, lq2, lk2, subln, w_out, batch, seq):
    a_q = NSA_HEADS * HEAD_DIM
    a_kv = NSA_KV * HEAD_DIM
    o2 = a_q + 6 * a_kv
    o3 = o2 + NSA_HEADS * N_BRANCH
    w_main = jnp.concatenate([w_in[:, :o2], w_in[:, o3:]], axis=1).astype(BF16)
    ng = NSA_GROUP * N_BRANCH
    w_gate = jnp.concatenate([jnp.pad(w_in[:, o2 + k * ng:o2 + (k + 1) * ng], ((0, 0), (0, LANES - ng)))
                              for k in range(NSA_KV)], axis=1).astype(BF16)
    scale = np.ones((1, N_MAIN_COLS), np.float32)
    scale[:, :a_q] = HEAD_DIM ** -0.5 * LOG2E
    scale[:, CB_QB * LANES:CB_KB * LANES] = DIFF_QK ** -0.5 * LOG2E
    g = g_mix.reshape(1, -1)
    proj, gates = norm_matmul(h, g, w_main, jnp.asarray(scale), BF16, w_side=w_gate)

    nb = seq // CMP_STRIDE
    x4 = proj[:, CB_KCMP * LANES:CB_KSEL * LANES].reshape(batch, nb, CMP_STRIDE, 2 * NSA_KV, HEAD_DIM)
    x4 = x4.transpose(0, 3, 1, 2, 4).reshape(batch, 2 * NSA_KV, nb, CMP_STRIDE * HEAD_DIM)
    flat = lambda a: a.reshape(1, CMP_STRIDE * HEAD_DIM)
    pelo = jnp.stack([flat(pe_k[:CMP_STRIDE]), flat(pe_v[:CMP_STRIDE])])
    pehi = jnp.stack([flat(pe_k[CMP_STRIDE:]), flat(pe_v[CMP_STRIDE:])])
    wflat = lambda a: a.reshape(CMP_STRIDE * HEAD_DIM, HEAD_DIM)
    w1lo = jnp.stack([wflat(w1_k[:CMP_STRIDE]), wflat(w1_v[:CMP_STRIDE])]).astype(BF16)
    w1hi = jnp.stack([wflat(w1_k[CMP_STRIDE:]), wflat(w1_v[CMP_STRIDE:])]).astype(BF16)
    w2 = jnp.stack([w2_k, w2_v]).astype(BF16)
    kvc = compress(x4, pelo, pehi, w1lo, w1hi, w2)

    selmap_t, expand = _sel_constants(seq)
    vct = kvc[:, NSA_KV:].transpose(0, 1, 3, 2)
    oc, selmask = nsa_cmp_select(proj, kvc, vct, gates, selmap_t, batch, seq)
    tq, tk = min(NSA_TQ, seq), min(NSA_TK, seq)
    vst = _transposed_blocks(proj, CB_VSEL, NSA_KV, batch, seq, tk)
    vwt = _transposed_blocks(proj, CB_VWIN, NSA_KV, batch, seq, min(CHAIN_W, tq))
    o_a = nsa_sel_win(proj, vst, vwt, selmask, expand, gates, oc, batch, seq, tq, tk)

    lam_init = 0.8 - 0.6 * math.exp(-0.3 * layer_idx)
    lam_params = jnp.stack([lq1, lk1, lq2, lk2]).astype(F32)
    tk = min(DIFF_TK, seq)
    vt = _transposed_blocks(proj, CB_VB, DIFF_HEADS, batch, seq, tk)
    o_b = diff_attention(proj, vt, lam_params, subln.reshape(DIFF_HEADS, HEAD_DIM, 1), lam_init, batch, seq,
                         min(DIFF_TQ, seq), tk)

    w_out = w_out.astype(BF16)
    return matmul_residual([o_a, o_b], [w_out[:a_q], w_out[a_q:]], h)


def _rec_layer(h, g_mix, w_in, conv_w, conv_b, wa, ba, wx, bx, lam_p, w_out, batch, seq):
    n = w_in.shape[1]
    proj = norm_matmul(h, g_mix.reshape(1, -1), w_in.astype(BF16), jnp.ones((1, n), F32), F32)
    row = lambda v: v.reshape(1, -1)
    hy = lru(proj, conv_w, row(conv_b), wa.astype(BF16), wx.astype(BF16), row(ba), row(bx), row(lam_p), batch, seq)
    return matmul_residual([hy], [w_out.astype(BF16)], h)


def kernel(x, p, g_mix, g_ffn, g_ple, g_final, w_in_attn, cmp_pe_k, cmp_w1_k, cmp_w2_k, cmp_pe_v, cmp_w1_v, cmp_w2_v, diff_lq1, diff_lk1, diff_lq2, diff_lk2, diff_subln, w_out_attn, w_in_rec, conv_w, conv_b, lru_wa, lru_ba, lru_wx, lru_bx, lru_lambda, w_out_rec, w_ffn_gate, w_ffn_up, w_ffn_down, w_ple_proj, w_ple_gate):
    batch, seq, d = x.shape
    depth = p.shape[0]
    h = x.reshape(batch * seq, d)
    for i in range(depth):
        j = i // 2
        if i % 2 == 0:
            h = _attn_layer(h, i, g_mix[i], w_in_attn[j], cmp_pe_k[j], cmp_w1_k[j], cmp_w2_k[j],
                            cmp_pe_v[j], cmp_w1_v[j], cmp_w2_v[j], diff_lq1[j], diff_lk1[j],
                            diff_lq2[j], diff_lk2[j], diff_subln[j], w_out_attn[j], batch, seq)
        else:
            h = _rec_layer(h, g_mix[i], w_in_rec[j], conv_w[j], conv_b[j], lru_wa[j], lru_ba[j],
                           lru_wx[j], lru_bx[j], lru_lambda[j], w_out_rec[j], batch, seq)
        h = ffn(h, g_ffn[i].reshape(1, -1), w_ffn_gate[i].astype(BF16), w_ffn_up[i].astype(BF16),
                w_ffn_down[i].astype(BF16))
        g_out = g_final.reshape(1, -1) if i == depth - 1 else None
        h = ple(h, g_ple[i].reshape(1, -1), p[i].reshape(batch * seq, -1),
                w_ple_gate[i].astype(BF16), w_ple_proj[i].astype(BF16), g_out)
    return h.reshape(batch, seq, d)
```

```python
import functools
import math

import numpy as np
import jax
import jax.numpy as jnp
from jax import lax
from jax.experimental import pallas as pl
from jax.experimental.pallas import tpu as pltpu

F32 = jnp.float32
BF16 = jnp.bfloat16

EPS = 1e-6
NEG_INF = -1e30
BIG = 1e6

NSA_HEADS = 8
NSA_KV = 2
NSA_GROUP = NSA_HEADS // NSA_KV
HEAD_DIM = 128
CMP_STRIDE = 16
CMP_LEN = 2 * CMP_STRIDE
SEL_LEN = 64
SEL_TOPK = 16
WIN = 512
N_BRANCH = 3
DIFF_HEADS = 8
DIFF_QK = 64
CONV_W = 4
LRU_BLOCKS = 8
LRU_C = 8.0

LANES = 128
SUBLANES = 8
VMEM_LIMIT = 56 * 1024 * 1024

CB_QA = 0
CB_KCMP, CB_VCMP = 8, 10
CB_KSEL, CB_VSEL = 12, 14
CB_KWIN, CB_VWIN = 16, 18
CB_QB, CB_KB, CB_VB = 20, 28, 36
N_MAIN_COLS = 44 * LANES

LOG2E = math.log2(math.e)
DIFF_TQ, DIFF_TK = 2048, 512
CHAIN_W = 256
NSA_TQ, NSA_TK = 1024, 512


def _cparams(*sem):
    return pltpu.CompilerParams(dimension_semantics=sem, vmem_limit_bytes=VMEM_LIMIT)


def _dot(a, b):
    return jnp.dot(a, b, preferred_element_type=F32)


def _dot_nt(a, b):
    return lax.dot_general(a, b, (((1,), (1,)), ((), ())), preferred_element_type=F32)


def _rms(x, g):
    return x * lax.rsqrt(jnp.mean(x * x, axis=-1, keepdims=True) + EPS) * g


def _norm_matmul_kernel(x_ref, g_ref, w_ref, s_ref, *rest):
    if len(rest) == 4:
        w_side_ref, o_ref, o_side_ref, xn_ref = rest
    else:
        (o_ref, xn_ref), w_side_ref, o_side_ref = rest, None, None

    @pl.when(pl.program_id(1) == 0)
    def _():
        xn_ref[...] = _rms(x_ref[...], g_ref[...]).astype(BF16)
        if w_side_ref is not None:
            o_side_ref[...] = _dot(xn_ref[...], w_side_ref[...])

    o_ref[...] = (_dot(xn_ref[...], w_ref[...]) * s_ref[...]).astype(o_ref.dtype)


def norm_matmul(x, g, w, col_scale, out_dtype, w_side=None, tm=1024, tn=512):
    m, d = x.shape
    n = w.shape[1]
    tm, tn = min(tm, m), min(tn, n)
    in_specs = [pl.BlockSpec((tm, d), lambda i, j: (i, 0)),
                pl.BlockSpec((1, d), lambda i, j: (0, 0)),
                pl.BlockSpec((d, tn), lambda i, j: (0, j)),
                pl.BlockSpec((1, tn), lambda i, j: (0, j))]
    out_specs = [pl.BlockSpec((tm, tn), lambda i, j: (i, j))]
    out_shape = [jax.ShapeDtypeStruct((m, n), out_dtype)]
    args = [x, g, w, col_scale]
    if w_side is not None:
        n2 = w_side.shape[1]
        in_specs.append(pl.BlockSpec((d, n2), lambda i, j: (0, 0)))
        out_specs.append(pl.BlockSpec((tm, n2), lambda i, j: (i, 0)))
        out_shape.append(jax.ShapeDtypeStruct((m, n2), F32))
        args.append(w_side)
    out = pl.pallas_call(
        _norm_matmul_kernel,
        grid=(m // tm, n // tn),
        in_specs=in_specs,
        out_specs=out_specs,
        out_shape=out_shape,
        scratch_shapes=[pltpu.VMEM((tm, d), BF16)],
        compiler_params=_cparams("parallel", "arbitrary"),
        name="norm_matmul",
    )(*args)
    return out if w_side is not None else out[0]


def _matmul_res_kernel(*refs, tn):
    n_ops = (len(refs) - 2) // 2
    res_ref, o_ref = refs[2 * n_ops], refs[2 * n_ops + 1]
    for c in range(o_ref.shape[1] // tn):
        cs = slice(c * tn, (c + 1) * tn)
        acc = res_ref[:, cs]
        for a_ref, w_ref in zip(refs[:n_ops], refs[n_ops:2 * n_ops]):
            acc = acc + _dot(a_ref[...], w_ref[:, cs])
        o_ref[:, cs] = acc


def matmul_residual(a_list, w_list, res, tm=512, tn=512):
    m, n = res.shape
    tm = min(tm, m)
    in_specs = [pl.BlockSpec((tm, a.shape[1]), lambda i: (i, 0)) for a in a_list]
    in_specs += [pl.BlockSpec(w.shape, lambda i: (0, 0)) for w in w_list]
    in_specs += [pl.BlockSpec((tm, n), lambda i: (i, 0))]
    return pl.pallas_call(
        functools.partial(_matmul_res_kernel, tn=tn),
        grid=(m // tm,),
        in_specs=in_specs,
        out_specs=pl.BlockSpec((tm, n), lambda i: (i, 0)),
        out_shape=jax.ShapeDtypeStruct((m, n), F32),
        compiler_params=_cparams("parallel"),
        name="matmul_residual",
    )(*a_list, *w_list, res)


def _ffn_kernel(x_ref, g_ref, wg_ref, wu_ref, wd_ref, o_ref, xn_ref, acc_ref):
    f = pl.program_id(1)

    @pl.when(f == 0)
    def _():
        xn_ref[...] = _rms(x_ref[...], g_ref[...]).astype(BF16)
        acc_ref[...] = jnp.zeros_like(acc_ref)

    xn = xn_ref[...]
    act = jax.nn.silu(_dot(xn, wg_ref[...])) * _dot(xn, wu_ref[...])
    acc_ref[...] += _dot(act.astype(BF16), wd_ref[...])

    @pl.when(f == pl.num_programs(1) - 1)
    def _():
        o_ref[...] = x_ref[...] + acc_ref[...]


def ffn(x, g, wg, wu, wd, tm=512, tf=512):
    m, d = x.shape
    dff = wg.shape[1]
    return pl.pallas_call(
        _ffn_kernel,
        grid=(m // tm, dff // tf),
        in_specs=[pl.BlockSpec((tm, d), lambda i, f: (i, 0)),
                  pl.BlockSpec((1, d), lambda i, f: (0, 0)),
                  pl.BlockSpec((d, tf), lambda i, f: (0, f)),
                  pl.BlockSpec((d, tf), lambda i, f: (0, f)),
                  pl.BlockSpec((tf, d), lambda i, f: (f, 0))],
        out_specs=pl.BlockSpec((tm, d), lambda i, f: (i, 0)),
        out_shape=jax.ShapeDtypeStruct((m, d), F32),
        scratch_shapes=[pltpu.VMEM((tm, d), BF16), pltpu.VMEM((tm, d), F32)],
        compiler_params=_cparams("parallel", "arbitrary"),
        name="ffn",
    )(x, g, wg, wu, wd)


def _ple_kernel(x_ref, g_ref, p_ref, wg_ref, wp_ref, *rest, tn):
    o_ref = rest[-1]
    xn = _rms(x_ref[...], g_ref[...]).astype(BF16)
    pb = p_ref[...].astype(BF16)
    for c in range(o_ref.shape[1] // tn):
        cs = slice(c * tn, (c + 1) * tn)
        gate = jax.nn.sigmoid(_dot(xn, wg_ref[:, cs]))
        o_ref[:, cs] = x_ref[:, cs] + gate * _dot(pb, wp_ref[:, cs])
    if len(rest) == 2:
        o_ref[...] = _rms(o_ref[...], rest[0][...])


def ple(x, g, p, wgate, wproj, g_out=None, tm=512, tn=512):
    m, d = x.shape
    tm = min(tm, m)
    pd = p.shape[1]
    in_specs = [pl.BlockSpec((tm, d), lambda i: (i, 0)),
                pl.BlockSpec((1, d), lambda i: (0, 0)),
                pl.BlockSpec((tm, pd), lambda i: (i, 0)),
                pl.BlockSpec((d, d), lambda i: (0, 0)),
                pl.BlockSpec((pd, d), lambda i: (0, 0))]
    args = [x, g, p, wgate, wproj]
    if g_out is not None:
        in_specs.append(pl.BlockSpec((1, d), lambda i: (0, 0)))
        args.append(g_out)
    return pl.pallas_call(
        functools.partial(_ple_kernel, tn=tn),
        grid=(m // tm,),
        in_specs=in_specs,
        out_specs=pl.BlockSpec((tm, d), lambda i: (i, 0)),
        out_shape=jax.ShapeDtypeStruct((m, d), F32),
        compiler_params=_cparams("parallel"),
        name="ple",
    )(*args)


def _compress_kernel(x_ref, pelo_ref, pehi_ref, w1lo_ref, w1hi_ref, w2_ref, o_ref):
    x = x_ref[...].astype(F32)
    nb = x.shape[0]
    lo = _dot((x + pelo_ref[...]).astype(BF16), w1lo_ref[...])
    hi = _dot((x + pehi_ref[...]).astype(BF16), w1hi_ref[...])
    z = lo + pltpu.roll(hi, shift=nb - 1, axis=0)
    y = _dot(jax.nn.gelu(z).astype(BF16), w2_ref[...])
    row = lax.broadcasted_iota(jnp.int32, y.shape, 0)
    o_ref[...] = jnp.where(row < nb - 1, y, 0.0).astype(o_ref.dtype)


def compress(x4, pelo, pehi, w1lo, w1hi, w2):
    b, _, nb, kd = x4.shape
    sel = lambda bi, c: (c // NSA_KV, 0, 0)
    return pl.pallas_call(
        _compress_kernel,
        grid=(b, 2 * NSA_KV),
        in_specs=[pl.BlockSpec((None, None, nb, kd), lambda bi, c: (bi, c, 0, 0)),
                  pl.BlockSpec((None, 1, kd), sel),
                  pl.BlockSpec((None, 1, kd), sel),
                  pl.BlockSpec((None, kd, HEAD_DIM), sel),
                  pl.BlockSpec((None, kd, HEAD_DIM), sel),
                  pl.BlockSpec((None, HEAD_DIM, HEAD_DIM), sel)],
        out_specs=pl.BlockSpec((None, None, nb, HEAD_DIM), lambda bi, c: (bi, c, 0, 0)),
        out_shape=jax.ShapeDtypeStruct((b, 2 * NSA_KV, nb, HEAD_DIM), BF16),
        compiler_params=_cparams("parallel", "parallel"),
        name="nsa_compress",
    )(x4, pelo, pehi, w1lo, w1hi, w2)


def _stack_heads(q):
    return jnp.concatenate([q[:, g * HEAD_DIM:(g + 1) * HEAD_DIM] for g in range(NSA_GROUP)], axis=0)


def _nsa_cmp_kernel(q_ref, kc_ref, vct_ref, gate_ref, selmap_t_ref, oc_ref, bias_ref, *, n_cmp, n_sel, topk):
    tq = q_ref.shape[0]
    s0 = pl.program_id(2) * tq
    q4 = _stack_heads(q_ref[...])
    st = _dot_nt(kc_ref[...], q4)
    t1 = s0 + lax.broadcasted_iota(jnp.int32, (1, tq), 1)
    t = jnp.concatenate([t1] * NSA_GROUP, axis=1)
    n = lax.broadcasted_iota(jnp.int32, (st.shape[0], 1), 0)
    valid = (n * CMP_STRIDE + (CMP_LEN - 1) <= t) & (n < n_cmp)
    sm = jnp.where(valid, st, NEG_INF)
    e = jnp.where(valid, jnp.exp2(sm - jnp.max(sm, axis=0, keepdims=True)), 0.0)
    l = jnp.sum(e, axis=0, keepdims=True)
    p = e * (1.0 / jnp.where(l > 0.0, l, 1.0))
    o_t = _dot(vct_ref[...], p.astype(BF16))

    gates_t = jax.nn.sigmoid(gate_ref[...]).T
    for g in range(NSA_GROUP):
        r = g * N_BRANCH
        oc_ref[:, g * HEAD_DIM:(g + 1) * HEAD_DIM] = (gates_t[r:r + 1] * o_t[:, g * tq:(g + 1) * tq]).T

    ps = p[:, 0:tq]
    for g in range(1, NSA_GROUP):
        ps = ps + p[:, g * tq:(g + 1) * tq]
    ps_hi = ps.astype(BF16)
    ps_lo = (ps - ps_hi.astype(F32)).astype(BF16)
    imp = _dot(selmap_t_ref[...], ps_hi) + _dot(selmap_t_ref[...], ps_lo)
    blk = lax.broadcasted_iota(jnp.int32, (LANES, 1), 0)
    cur = t1 // SEL_LEN
    forced = (blk == 0) | (blk == cur) | (blk == cur - 1)
    imp = jnp.where(blk > cur, -BIG, jnp.where(forced, BIG, imp))
    imp = jnp.where(blk < n_sel, imp, -jnp.inf)
    sel = jnp.zeros(imp.shape, F32)
    blk_f = blk.astype(F32)
    for _ in range(topk):
        mx = jnp.max(imp, axis=0, keepdims=True)
        idx = jnp.min(jnp.where(imp == mx, blk_f, float(LANES)), axis=0, keepdims=True)
        pick = blk_f == idx
        sel = jnp.where(pick, 1.0, sel)
        imp = jnp.where(pick, -jnp.inf, imp)
    bias_ref[...] = jnp.where(sel > 0.5, 0.0, NEG_INF).T.astype(bias_ref.dtype)


def nsa_cmp_select(proj, kvc, vct, gates, selmap_t, batch, seq, tq=256):
    nb = kvc.shape[2]
    tq = min(tq, seq)
    nq = seq // tq
    n_sel = seq // SEL_LEN
    kern = functools.partial(_nsa_cmp_kernel, n_cmp=nb - 1, n_sel=n_sel, topk=min(SEL_TOPK, n_sel))
    gw = NSA_GROUP * HEAD_DIM
    return pl.pallas_call(
        kern,
        grid=(batch, NSA_KV, nq),
        in_specs=[pl.BlockSpec((tq, gw), lambda b, h, i: (b * nq + i, h)),
                  pl.BlockSpec((None, None, nb, HEAD_DIM), lambda b, h, i: (b, h, 0, 0)),
                  pl.BlockSpec((None, None, HEAD_DIM, nb), lambda b, h, i: (b, h, 0, 0)),
                  pl.BlockSpec((tq, LANES), lambda b, h, i: (b * nq + i, h)),
                  pl.BlockSpec((LANES, nb), lambda b, h, i: (0, 0))],
        out_specs=[pl.BlockSpec((tq, gw), lambda b, h, i: (b * nq + i, h)),
                   pl.BlockSpec((None, None, tq, LANES), lambda b, h, i: (b, h, i, 0))],
        out_shape=[jax.ShapeDtypeStruct((batch * seq, NSA_HEADS * HEAD_DIM), F32),
                   jax.ShapeDtypeStruct((batch, NSA_KV, seq, LANES), BF16)],
        compiler_params=_cparams("parallel", "parallel", "arbitrary"),
        name="nsa_cmp_select",
    )(proj, kvc, vct, gates, selmap_t)


def _flash_update_t(st, vt_blk, m_ref, l_ref, acc_ref, c, valid):
    if valid is not None:
        st = jnp.where(valid, st, NEG_INF)
    m_prev = m_ref[c]
    m_new = jnp.maximum(m_prev, jnp.max(st, axis=0, keepdims=True))
    p = jnp.exp2(st - m_new)
    alpha = jnp.exp2(m_prev - m_new)
    l_ref[c] = alpha * l_ref[c] + jnp.sum(p, axis=0, keepdims=True)
    acc_ref[c] = alpha * acc_ref[c] + _dot(vt_blk, p.astype(BF16))
    m_ref[c] = m_new


def _causal_chains(diag, nj, cw, tk):
    if diag is None:
        return [(j, False) for j in range(nj)]
    lo, hi = diag * tk, (diag + 1) * tk - 1
    return [(j, hi > j * cw) for j in range(nj) if lo <= (j + 1) * cw - 1]


def _nsa_sel_win_kernel(q_ref, ks_ref, vst_ref, kw_ref, vwt_ref, bias_ref, expand_ref, gate_ref, oc_ref,
                        o_ref, m_ref, l_ref, acc_ref, *, tk):
    tq = q_ref.shape[0]
    cw = acc_ref.shape[2]
    nj = tq // cw
    s0 = pl.program_id(2) * tq
    q = q_ref[...]
    qc = [[q[j * cw:(j + 1) * cw, g * HEAD_DIM:(g + 1) * HEAD_DIM] for j in range(nj)] for g in range(NSA_GROUP)]
    t = s0 + lax.broadcasted_iota(jnp.int32, (1, tq), 1)
    bias = bias_ref[...]
    qx = [[jnp.concatenate([qc[g][j], bias[j * cw:(j + 1) * cw]], axis=1) for j in range(nj)]
          for g in range(NSA_GROUP)]

    m_ref[...] = jnp.full(m_ref.shape, NEG_INF, F32)
    l_ref[...] = jnp.zeros(l_ref.shape, F32)
    acc_ref[...] = jnp.zeros(acc_ref.shape, F32)

    def step(kb, diag):
        k0 = pl.multiple_of(kb * tk, tk)
        kx = jnp.concatenate([ks_ref[pl.ds(k0, tk), :], expand_ref[pl.ds(k0, tk), :]], axis=1)
        vt_blk = vst_ref[kb]
        todo = _causal_chains(diag, nj, cw, tk)
        scores = {(g, j): _dot_nt(kx, qx[g][j]) for j, _ in todo for g in range(NSA_GROUP)}
        for j, masked in todo:
            valid = None
            if masked:
                valid = k0 + lax.broadcasted_iota(jnp.int32, (tk, 1), 0) <= t[:, j * cw:(j + 1) * cw]
            for g in range(NSA_GROUP):
                _flash_update_t(scores[(g, j)], vt_blk, m_ref, l_ref, acc_ref, g * nj + j, valid)

    n_full = s0 // tk

    def full_body(kb, c):
        step(kb, None)
        return c

    lax.fori_loop(0, n_full, full_body, 0)
    for d in range(tq // tk):
        step(n_full + d, d)

    nwb = WIN // cw + 1
    gates_t = jax.nn.sigmoid(gate_ref[...]).T
    for j in range(nj):
        j0 = jnp.maximum(s0 // cw + j - WIN // cw, 0)
        w0 = pl.multiple_of(j0 * cw, cw)
        kwblk = kw_ref[pl.ds(w0, nwb * cw), :]
        tj = t[:, j * cw:(j + 1) * cw]
        dlt = tj - (w0 + lax.broadcasted_iota(jnp.int32, (nwb * cw, 1), 0))
        inwin = (dlt >= 0) & (dlt < WIN)
        wscores = [_dot_nt(kwblk, qc[g][j]) for g in range(NSA_GROUP)]
        for g in range(NSA_GROUP):
            sw = jnp.where(inwin, wscores[g], NEG_INF)
            pw = jnp.exp2(sw - jnp.max(sw, axis=0, keepdims=True))
            lw = jnp.sum(pw, axis=0, keepdims=True)
            pw = pw.astype(BF16)
            o_win = _dot(vwt_ref[j0], pw[0:cw])
            for jj in range(1, nwb):
                o_win = o_win + _dot(vwt_ref[j0 + jj], pw[jj * cw:(jj + 1) * cw])
            r = g * N_BRANCH
            c = g * nj + j
            g_sel = gates_t[r + 1:r + 2, j * cw:(j + 1) * cw]
            g_win = gates_t[r + 2:r + 3, j * cw:(j + 1) * cw]
            o_t = g_sel * (acc_ref[c] / l_ref[c]) + g_win * (o_win / lw)
            rs, cs = slice(j * cw, (j + 1) * cw), slice(g * HEAD_DIM, (g + 1) * HEAD_DIM)
            o_ref[rs, cs] = (oc_ref[rs, cs] + o_t.T).astype(o_ref.dtype)


def nsa_sel_win(proj, vst, vwt, selbias, expand, gates, oc, batch, seq, tq, tk):
    nq = seq // tq
    cw = vwt.shape[-1]
    nch = NSA_GROUP * tq // cw
    gw = NSA_GROUP * HEAD_DIM
    kv_spec = lambda cb: pl.BlockSpec((seq, HEAD_DIM), lambda b, h, i: (b, cb + h))
    vt_spec = lambda blk: pl.BlockSpec((None, None, seq // blk, HEAD_DIM, blk), lambda b, h, i: (b, h, 0, 0, 0))
    return pl.pallas_call(
        functools.partial(_nsa_sel_win_kernel, tk=tk),
        grid=(batch, NSA_KV, nq),
        in_specs=[pl.BlockSpec((tq, gw), lambda b, h, i: (b * nq + i, h)),
                  kv_spec(CB_KSEL), vt_spec(tk), kv_spec(CB_KWIN), vt_spec(cw),
                  pl.BlockSpec((None, None, tq, LANES), lambda b, h, i: (b, h, i, 0)),
                  pl.BlockSpec((seq, LANES), lambda b, h, i: (0, 0)),
                  pl.BlockSpec((tq, LANES), lambda b, h, i: (b * nq + i, h)),
                  pl.BlockSpec((tq, gw), lambda b, h, i: (b * nq + i, h))],
        out_specs=pl.BlockSpec((tq, gw), lambda b, h, i: (b * nq + i, h)),
        out_shape=jax.ShapeDtypeStruct((batch * seq, NSA_HEADS * HEAD_DIM), BF16),
        scratch_shapes=[pltpu.VMEM((nch, 1, cw), F32), pltpu.VMEM((nch, 1, cw), F32),
                        pltpu.VMEM((nch, HEAD_DIM, cw), F32)],
        compiler_params=_cparams("parallel", "parallel", "arbitrary"),
        name="nsa_sel_win",
    )(proj, proj, vst, proj, vwt, selbias, expand, gates, oc)


def _diff_attn_kernel(q_ref, k_ref, vt_ref, lamp_ref, subln_ref, o_ref, m_ref, l_ref, acc_ref, *, tk, lam_init):
    tq = q_ref.shape[0]
    s0 = pl.program_id(2) * tq
    q = q_ref[...]
    lane = lax.broadcasted_iota(jnp.int32, (1, LANES), 1)
    zero = jnp.zeros_like(q)
    qmaps = (jnp.where(lane < DIFF_QK, q, zero), jnp.where(lane >= DIFF_QK, q, zero))
    cw = acc_ref.shape[2]
    nj = tq // cw
    t = s0 + lax.broadcasted_iota(jnp.int32, (1, tq), 1)

    m_ref[...] = jnp.full(m_ref.shape, NEG_INF, F32)
    l_ref[...] = jnp.zeros(l_ref.shape, F32)
    acc_ref[...] = jnp.zeros(acc_ref.shape, F32)

    def step(kb, diag):
        k0 = pl.multiple_of(kb * tk, tk)
        kblk = k_ref[pl.ds(k0, tk), :]
        vt_blk = vt_ref[kb]
        todo = _causal_chains(diag, nj, cw, tk)
        chains = [(c, j, masked) for j, masked in todo for c in range(2)]
        scores = [_dot_nt(kblk, qmaps[c][j * cw:(j + 1) * cw]) for c, j, _ in chains]
        for (c, j, masked), st in zip(chains, scores):
            valid = None
            if masked:
                valid = k0 + lax.broadcasted_iota(jnp.int32, (tk, 1), 0) <= t[:, j * cw:(j + 1) * cw]
            _flash_update_t(st, vt_blk, m_ref, l_ref, acc_ref, c * nj + j, valid)

    n_full = s0 // tk

    def full_body(kb, c):
        step(kb, None)
        return c

    lax.fori_loop(0, n_full, full_body, 0)
    for d in range(tq // tk):
        step(n_full + d, d)

    lp = lamp_ref[...]
    lam = (jnp.exp(jnp.sum(lp[0:1] * lp[1:2], axis=-1, keepdims=True))
           - jnp.exp(jnp.sum(lp[2:3] * lp[3:4], axis=-1, keepdims=True)) + lam_init)
    for j in range(nj):
        a = acc_ref[j] / l_ref[j] - lam * (acc_ref[nj + j] / l_ref[nj + j])
        y = a * lax.rsqrt(jnp.mean(a * a, axis=0, keepdims=True) + EPS) * subln_ref[...]
        o_ref[j * cw:(j + 1) * cw, :] = (y * (1.0 - lam_init)).T.astype(o_ref.dtype)


def diff_attention(proj, vt, lam_params, subln, lam_init, batch, seq, tq, tk):
    nq = seq // tq
    cw = min(CHAIN_W, tq)
    return pl.pallas_call(
        functools.partial(_diff_attn_kernel, tk=tk, lam_init=lam_init),
        grid=(batch, DIFF_HEADS, nq),
        in_specs=[pl.BlockSpec((tq, HEAD_DIM), lambda b, h, i: (b * nq + i, CB_QB + h)),
                  pl.BlockSpec((seq, HEAD_DIM), lambda b, h, i: (b, CB_KB + h)),
                  pl.BlockSpec((None, None, seq // tk, HEAD_DIM, tk), lambda b, h, i: (b, h, 0, 0, 0)),
                  pl.BlockSpec((4, DIFF_QK), lambda b, h, i: (0, 0)),
                  pl.BlockSpec((None, HEAD_DIM, 1), lambda b, h, i: (h, 0, 0))],
        out_specs=pl.BlockSpec((tq, HEAD_DIM), lambda b, h, i: (b * nq + i, h)),
        out_shape=jax.ShapeDtypeStruct((batch * seq, DIFF_HEADS * HEAD_DIM), BF16),
        scratch_shapes=[pltpu.VMEM((2 * tq // cw, 1, cw), F32), pltpu.VMEM((2 * tq // cw, 1, cw), F32),
                        pltpu.VMEM((2 * tq // cw, HEAD_DIM, cw), F32)],
        compiler_params=_cparams("parallel", "parallel", "arbitrary"),
        name="diff_attention",
    )(proj, proj, vt, lam_params, subln)


def _lru_kernel(xb_ref, yb_ref, cw_ref, cb_ref, wa_ref, wx_ref, ba_ref, bx_ref, lam_ref, o_ref,
                xpad_ref, a_ref, u_ref, h_ref, carry_ref):
    tt = xb_ref.shape[0]
    pad = SUBLANES

    @pl.when(pl.program_id(2) == 0)
    def _():
        xpad_ref[0:pad, :] = jnp.zeros((pad, xpad_ref.shape[1]), F32)
        carry_ref[...] = jnp.zeros_like(carry_ref)

    xb = xb_ref[...]
    xpad_ref[pad:pad + tt, :] = xb
    cw = cw_ref[...]
    xc = cb_ref[...] + cw[CONV_W - 1:CONV_W] * xb
    for w in range(CONV_W - 1):
        off = pad - (CONV_W - 1) + w
        xc = xc + cw[w:w + 1] * xpad_ref[off:off + tt, :]
    xpad_ref[0:pad, :] = xb[tt - pad:tt]

    xcb = xc.astype(BF16)
    r = jax.nn.sigmoid(_dot(xcb, wa_ref[...]) + ba_ref[...])
    i = jax.nn.sigmoid(_dot(xcb, wx_ref[...]) + bx_ref[...])
    nl = -lam_ref[...]
    softplus = jnp.maximum(nl, 0.0) + jnp.log1p(jnp.exp(-jnp.abs(nl)))
    log_a = -LRU_C * r * softplus
    a = jnp.exp(log_a)
    a_ref[...] = a
    u_ref[...] = jnp.sqrt(1.0 - a * a) * (i * xc)

    row = lax.broadcasted_iota(jnp.int32, (SUBLANES, a.shape[1]), 0)

    def slab(j, hprev):
        r0 = pl.multiple_of(j * SUBLANES, SUBLANES)
        aa = a_ref[pl.ds(r0, SUBLANES), :]
        bb = u_ref[pl.ds(r0, SUBLANES), :]
        for d in (1, 2, 4):
            a_sh = jnp.where(row >= d, pltpu.roll(aa, shift=d, axis=0), 1.0)
            b_sh = jnp.where(row >= d, pltpu.roll(bb, shift=d, axis=0), 0.0)
            bb = aa * b_sh + bb
            aa = aa * a_sh
        hh = bb + aa * hprev
        h_ref[pl.ds(r0, SUBLANES), :] = hh
        return hh[SUBLANES - 1:SUBLANES, :]

    carry_ref[...] = lax.fori_loop(0, tt // SUBLANES, slab, carry_ref[...], unroll=4)
    o_ref[...] = (h_ref[...] * jax.nn.gelu(yb_ref[...])).astype(o_ref.dtype)


def lru(proj, conv_w, conv_b, wa, wx, ba, bx, lam, batch, seq, tt=512):
    w = proj.shape[1] // 2
    bw = w // LRU_BLOCKS
    nt = seq // tt
    vec = lambda: pl.BlockSpec((1, bw), lambda b, n, t: (0, n))
    return pl.pallas_call(
        _lru_kernel,
        grid=(batch, LRU_BLOCKS, nt),
        in_specs=[pl.BlockSpec((tt, bw), lambda b, n, t: (b * nt + t, n)),
                  pl.BlockSpec((tt, bw), lambda b, n, t: (b * nt + t, LRU_BLOCKS + n)),
                  pl.BlockSpec((CONV_W, bw), lambda b, n, t: (0, n)),
                  vec(),
                  pl.BlockSpec((None, bw, bw), lambda b, n, t: (n, 0, 0)),
                  pl.BlockSpec((None, bw, bw), lambda b, n, t: (n, 0, 0)),
                  vec(), vec(), vec()],
        out_specs=pl.BlockSpec((tt, bw), lambda b, n, t: (b * nt + t, n)),
        out_shape=jax.ShapeDtypeStruct((batch * seq, w), BF16),
        scratch_shapes=[pltpu.VMEM((tt + SUBLANES, bw), F32), pltpu.VMEM((tt, bw), F32),
                        pltpu.VMEM((tt, bw), F32), pltpu.VMEM((tt, bw), F32), pltpu.VMEM((1, bw), F32)],
        compiler_params=_cparams("parallel", "parallel", "arbitrary"),
        name="rglru",
    )(proj, proj, conv_w, conv_b, wa, wx, ba, bx, lam)


def _sel_constants(seq):
    n_blk = seq // CMP_STRIDE
    n_sel = seq // SEL_LEN
    c0 = np.arange(n_blk)[:, None] * CMP_STRIDE
    j0 = np.arange(LANES)[None, :] * SEL_LEN
    ov = np.clip(np.minimum(c0 + CMP_LEN, j0 + SEL_LEN) - np.maximum(c0, j0), 0, None) / CMP_LEN
    ov[n_blk - 1:, :] = 0.0
    ov[:, n_sel:] = 0.0
    expand = (np.arange(seq)[:, None] // SEL_LEN == np.arange(LANES)[None, :]).astype(np.float32)
    return jnp.asarray(ov.T, BF16), jnp.asarray(expand, BF16)


def _transposed_blocks(proj, col_block, n_heads, batch, seq, blk):
    v = proj[:, col_block * LANES:(col_block + n_heads) * LANES].reshape(batch, seq // blk, blk, n_heads, HEAD_DIM)
    return v.transpose(0, 3, 1, 4, 2)


def _attn_layer(h, layer_idx, g_mix, w_in, pe_k, w1_k, w2_k, pe_v, w1_v, w2_v,
                lq1, lk1, lq2, lk2, subln, w_out, batch, seq):
    a_q = NSA_HEADS * HEAD_DIM
    a_kv = NSA_KV * HEAD_DIM
    o2 = a_q + 6 * a_kv
    o3 = o2 + NSA_HEADS * N_BRANCH
    w_main = jnp.concatenate([w_in[:, :o2], w_in[:, o3:]], axis=1).astype(BF16)
    ng = NSA_GROUP * N_BRANCH
    w_gate = jnp.concatenate([jnp.pad(w_in[:, o2 + k * ng:o2 + (k + 1) * ng], ((0, 0), (0, LANES - ng)))
                              for k in range(NSA_KV)], axis=1).astype(BF16)
    scale = np.ones((1, N_MAIN_COLS), np.float32)
    scale[:, :a_q] = HEAD_DIM ** -0.5 * LOG2E
    scale[:, CB_QB * LANES:CB_KB * LANES] = DIFF_QK ** -0.5 * LOG2E
    g = g_mix.reshape(1, -1)
    proj, gates = norm_matmul(h, g, w_main, jnp.asarray(scale), BF16, w_side=w_gate)

    nb = seq // CMP_STRIDE
    x4 = proj[:, CB_KCMP * LANES:CB_KSEL * LANES].reshape(batch, nb, CMP_STRIDE, 2 * NSA_KV, HEAD_DIM)
    x4 = x4.transpose(0, 3, 1, 2, 4).reshape(batch, 2 * NSA_KV, nb, CMP_STRIDE * HEAD_DIM)
    flat = lambda a: a.reshape(1, CMP_STRIDE * HEAD_DIM)
    pelo = jnp.stack([flat(pe_k[:CMP_STRIDE]), flat(pe_v[:CMP_STRIDE])])
    pehi = jnp.stack([flat(pe_k[CMP_STRIDE:]), flat(pe_v[CMP_STRIDE:])])
    wflat = lambda a: a.reshape(CMP_STRIDE * HEAD_DIM, HEAD_DIM)
    w1lo = jnp.stack([wflat(w1_k[:CMP_STRIDE]), wflat(w1_v[:CMP_STRIDE])]).astype(BF16)
    w1hi = jnp.stack([wflat(w1_k[CMP_STRIDE:]), wflat(w1_v[CMP_STRIDE:])]).astype(BF16)
    w2 = jnp.stack([w2_k, w2_v]).astype(BF16)
    kvc = compress(x4, pelo, pehi, w1lo, w1hi, w2)

    selmap_t, expand = _sel_constants(seq)
    vct = kvc[:, NSA_KV:].transpose(0, 1, 3, 2)
    oc, selbias = nsa_cmp_select(proj, kvc, vct, gates, selmap_t, batch, seq)
    tq, tk = min(NSA_TQ, seq), min(NSA_TK, seq)
    vst = _transposed_blocks(proj, CB_VSEL, NSA_KV, batch, seq, tk)
    vwt = _transposed_blocks(proj, CB_VWIN, NSA_KV, batch, seq, min(CHAIN_W, tq))
    o_a = nsa_sel_win(proj, vst, vwt, selbias, expand, gates, oc, batch, seq, tq, tk)

    lam_init = 0.8 - 0.6 * math.exp(-0.3 * layer_idx)
    lam_params = jnp.stack([lq1, lk1, lq2, lk2]).astype(F32)
    tk = min(DIFF_TK, seq)
    vt = _transposed_blocks(proj, CB_VB, DIFF_HEADS, batch, seq, tk)
    o_b = diff_attention(proj, vt, lam_params, subln.reshape(DIFF_HEADS, HEAD_DIM, 1), lam_init, batch, seq,
                         min(DIFF_TQ, seq), tk)

    w_out = w_out.astype(BF16)
    return matmul_residual([o_a, o_b], [w_out[:a_q], w_out[a_q:]], h)


def _rec_layer(h, g_mix, w_in, conv_w, conv_b, wa, ba, wx, bx, lam_p, w_out, batch, seq):
    n = w_in.shape[1]
    proj = norm_matmul(h, g_mix.reshape(1, -1), w_in.astype(BF16), jnp.ones((1, n), F32), F32)
    row = lambda v: v.reshape(1, -1)
    hy = lru(proj, conv_w, row(conv_b), wa.astype(BF16), wx.astype(BF16), row(ba), row(bx), row(lam_p), batch, seq)
    return matmul_residual([hy], [w_out.astype(BF16)], h)


def kernel(x, p, g_mix, g_ffn, g_ple, g_final, w_in_attn, cmp_pe_k, cmp_w1_k, cmp_w2_k, cmp_pe_v, cmp_w1_v, cmp_w2_v, diff_lq1, diff_lk1, diff_lq2, diff_lk2, diff_subln, w_out_attn, w_in_rec, conv_w, conv_b, lru_wa, lru_ba, lru_wx, lru_bx, lru_lambda, w_out_rec, w_ffn_gate, w_ffn_up, w_ffn_down, w_ple_proj, w_ple_gate):
    batch, seq, d = x.shape
    depth = p.shape[0]
    h = x.reshape(batch * seq, d)
    for i in range(depth):
        j = i // 2
        if i % 2 == 0:
            h = _attn_layer(h, i, g_mix[i], w_in_attn[j], cmp_pe_k[j], cmp_w1_k[j], cmp_w2_k[j],
                            cmp_pe_v[j], cmp_w1_v[j], cmp_w2_v[j], diff_lq1[j], diff_lk1[j],
                            diff_lq2[j], diff_lk2[j], diff_subln[j], w_out_attn[j], batch, seq)
        else:
            h = _rec_layer(h, g_mix[i], w_in_rec[j], conv_w[j], conv_b[j], lru_wa[j], lru_ba[j],
                           lru_wx[j], lru_bx[j], lru_lambda[j], w_out_rec[j], batch, seq)
        h = ffn(h, g_ffn[i].reshape(1, -1), w_ffn_gate[i].astype(BF16), w_ffn_up[i].astype(BF16),
                w_ffn_down[i].astype(BF16))
        g_out = g_final.reshape(1, -1) if i == depth - 1 else None
        h = ple(h, g_ple[i].reshape(1, -1), p[i].reshape(batch * seq, -1),
                w_ple_gate[i].astype(BF16), w_ple_proj[i].astype(BF16), g_out)
    return h.reshape(batch, seq, d)
```

```python
import functools
import math

import numpy as np
import jax
import jax.numpy as jnp
from jax import lax
from jax.experimental import pallas as pl
from jax.experimental.pallas import tpu as pltpu

F32 = jnp.float32
BF16 = jnp.bfloat16

EPS = 1e-6
NEG_INF = -1e30
BIG = 1e6

NSA_HEADS = 8
NSA_KV = 2
NSA_GROUP = NSA_HEADS // NSA_KV
HEAD_DIM = 128
CMP_STRIDE = 16
CMP_LEN = 2 * CMP_STRIDE
SEL_LEN = 64
SEL_TOPK = 16
WIN = 512
N_BRANCH = 3
DIFF_HEADS = 8
DIFF_QK = 64
CONV_W = 4
LRU_BLOCKS = 8
LRU_C = 8.0

LANES = 128
SUBLANES = 8
VMEM_LIMIT = 56 * 1024 * 1024

CB_QA = 0
CB_KCMP, CB_VCMP = 8, 10
CB_KSEL, CB_VSEL = 12, 14
CB_KWIN, CB_VWIN = 16, 18
CB_QB, CB_KB, CB_VB = 20, 28, 36
N_MAIN_COLS = 44 * LANES

LOG2E = math.log2(math.e)
DIFF_TQ, DIFF_TK = 2048, 512
CHAIN_W = 256
NSA_TQ, NSA_TK = 1024, 512


def _cparams(*sem):
    return pltpu.CompilerParams(dimension_semantics=sem, vmem_limit_bytes=VMEM_LIMIT)


def _dot(a, b):
    return jnp.dot(a, b, preferred_element_type=F32)


def _dot_nt(a, b):
    return lax.dot_general(a, b, (((1,), (1,)), ((), ())), preferred_element_type=F32)


def _rms(x, g):
    return x * lax.rsqrt(jnp.mean(x * x, axis=-1, keepdims=True) + EPS) * g


def _norm_matmul_kernel(x_ref, g_ref, w_ref, s_ref, *rest):
    if len(rest) == 4:
        w_side_ref, o_ref, o_side_ref, xn_ref = rest
    else:
        (o_ref, xn_ref), w_side_ref, o_side_ref = rest, None, None

    @pl.when(pl.program_id(1) == 0)
    def _():
        xn_ref[...] = _rms(x_ref[...], g_ref[...]).astype(BF16)
        if w_side_ref is not None:
            o_side_ref[...] = _dot(xn_ref[...], w_side_ref[...])

    o_ref[...] = (_dot(xn_ref[...], w_ref[...]) * s_ref[...]).astype(o_ref.dtype)


def norm_matmul(x, g, w, col_scale, out_dtype, w_side=None, tm=1024, tn=512):
    m, d = x.shape
    n = w.shape[1]
    tm, tn = min(tm, m), min(tn, n)
    in_specs = [pl.BlockSpec((tm, d), lambda i, j: (i, 0)),
                pl.BlockSpec((1, d), lambda i, j: (0, 0)),
                pl.BlockSpec((d, tn), lambda i, j: (0, j)),
                pl.BlockSpec((1, tn), lambda i, j: (0, j))]
    out_specs = [pl.BlockSpec((tm, tn), lambda i, j: (i, j))]
    out_shape = [jax.ShapeDtypeStruct((m, n), out_dtype)]
    args = [x, g, w, col_scale]
    if w_side is not None:
        n2 = w_side.shape[1]
        in_specs.append(pl.BlockSpec((d, n2), lambda i, j: (0, 0)))
        out_specs.append(pl.BlockSpec((tm, n2), lambda i, j: (i, 0)))
        out_shape.append(jax.ShapeDtypeStruct((m, n2), F32))
        args.append(w_side)
    out = pl.pallas_call(
        _norm_matmul_kernel,
        grid=(m // tm, n // tn),
        in_specs=in_specs,
        out_specs=out_specs,
        out_shape=out_shape,
        scratch_shapes=[pltpu.VMEM((tm, d), BF16)],
        compiler_params=_cparams("parallel", "arbitrary"),
        name="norm_matmul",
    )(*args)
    return out if w_side is not None else out[0]


def _matmul_res_kernel(*refs, tn):
    n_ops = (len(refs) - 2) // 2
    res_ref, o_ref = refs[2 * n_ops], refs[2 * n_ops + 1]
    for c in range(o_ref.shape[1] // tn):
        cs = slice(c * tn, (c + 1) * tn)
        acc = res_ref[:, cs]
        for a_ref, w_ref in zip(refs[:n_ops], refs[n_ops:2 * n_ops]):
            acc = acc + _dot(a_ref[...], w_ref[:, cs])
        o_ref[:, cs] = acc


def matmul_residual(a_list, w_list, res, tm=512, tn=512):
    m, n = res.shape
    tm = min(tm, m)
    in_specs = [pl.BlockSpec((tm, a.shape[1]), lambda i: (i, 0)) for a in a_list]
    in_specs += [pl.BlockSpec(w.shape, lambda i: (0, 0)) for w in w_list]
    in_specs += [pl.BlockSpec((tm, n), lambda i: (i, 0))]
    return pl.pallas_call(
        functools.partial(_matmul_res_kernel, tn=tn),
        grid=(m // tm,),
        in_specs=in_specs,
        out_specs=pl.BlockSpec((tm, n), lambda i: (i, 0)),
        out_shape=jax.ShapeDtypeStruct((m, n), F32),
        compiler_params=_cparams("parallel"),
        name="matmul_residual",
    )(*a_list, *w_list, res)


def _ffn_kernel(x_ref, g_ref, wg_ref, wu_ref, wd_ref, o_ref, xn_ref):
    f = pl.program_id(1)

    @pl.when(f == 0)
    def _():
        x = x_ref[...]
        xn_ref[...] = _rms(x, g_ref[...]).astype(BF16)
        o_ref[...] = x

    xn = xn_ref[...]
    act = jax.nn.silu(_dot(xn, wg_ref[...])) * _dot(xn, wu_ref[...])
    o_ref[...] += _dot(act.astype(BF16), wd_ref[...])


def ffn(x, g, wg, wu, wd, tm=1024, tf=256):
    m, d = x.shape
    dff = wg.shape[1]
    return pl.pallas_call(
        _ffn_kernel,
        grid=(m // tm, dff // tf),
        in_specs=[pl.BlockSpec((tm, d), lambda i, f: (i, 0)),
                  pl.BlockSpec((1, d), lambda i, f: (0, 0)),
                  pl.BlockSpec((d, tf), lambda i, f: (0, f)),
                  pl.BlockSpec((d, tf), lambda i, f: (0, f)),
                  pl.BlockSpec((tf, d), lambda i, f: (f, 0))],
        out_specs=pl.BlockSpec((tm, d), lambda i, f: (i, 0)),
        out_shape=jax.ShapeDtypeStruct((m, d), F32),
        scratch_shapes=[pltpu.VMEM((tm, d), BF16)],
        compiler_params=_cparams("parallel", "arbitrary"),
        name="ffn",
    )(x, g, wg, wu, wd)


def _ple_kernel(x_ref, g_ref, p_ref, wg_ref, wp_ref, *rest, tn):
    o_ref = rest[-1]
    xn = _rms(x_ref[...], g_ref[...]).astype(BF16)
    pb = p_ref[...].astype(BF16)
    for c in range(o_ref.shape[1] // tn):
        cs = slice(c * tn, (c + 1) * tn)
        gate = jax.nn.sigmoid(_dot(xn, wg_ref[:, cs]))
        o_ref[:, cs] = x_ref[:, cs] + gate * _dot(pb, wp_ref[:, cs])
    if len(rest) == 2:
        o_ref[...] = _rms(o_ref[...], rest[0][...])


def ple(x, g, p, wgate, wproj, g_out=None, tm=512, tn=512):
    m, d = x.shape
    tm = min(tm, m)
    pd = p.shape[1]
    in_specs = [pl.BlockSpec((tm, d), lambda i: (i, 0)),
                pl.BlockSpec((1, d), lambda i: (0, 0)),
                pl.BlockSpec((tm, pd), lambda i: (i, 0)),
                pl.BlockSpec((d, d), lambda i: (0, 0)),
                pl.BlockSpec((pd, d), lambda i: (0, 0))]
    args = [x, g, p, wgate, wproj]
    if g_out is not None:
        in_specs.append(pl.BlockSpec((1, d), lambda i: (0, 0)))
        args.append(g_out)
    return pl.pallas_call(
        functools.partial(_ple_kernel, tn=tn),
        grid=(m // tm,),
        in_specs=in_specs,
        out_specs=pl.BlockSpec((tm, d), lambda i: (i, 0)),
        out_shape=jax.ShapeDtypeStruct((m, d), F32),
        compiler_params=_cparams("parallel"),
        name="ple",
    )(*args)


def _compress_kernel(x_ref, pelo_ref, pehi_ref, w1lo_ref, w1hi_ref, w2_ref, o_ref):
    x = x_ref[...].astype(F32)
    nb = x.shape[0]
    lo = _dot((x + pelo_ref[...]).astype(BF16), w1lo_ref[...])
    hi = _dot((x + pehi_ref[...]).astype(BF16), w1hi_ref[...])
    z = lo + pltpu.roll(hi, shift=nb - 1, axis=0)
    y = _dot(jax.nn.gelu(z).astype(BF16), w2_ref[...])
    row = lax.broadcasted_iota(jnp.int32, y.shape, 0)
    o_ref[...] = jnp.where(row < nb - 1, y, 0.0).astype(o_ref.dtype)


def compress(x4, pelo, pehi, w1lo, w1hi, w2):
    b, _, nb, kd = x4.shape
    sel = lambda bi, c: (c // NSA_KV, 0, 0)
    return pl.pallas_call(
        _compress_kernel,
        grid=(b, 2 * NSA_KV),
        in_specs=[pl.BlockSpec((None, None, nb, kd), lambda bi, c: (bi, c, 0, 0)),
                  pl.BlockSpec((None, 1, kd), sel),
                  pl.BlockSpec((None, 1, kd), sel),
                  pl.BlockSpec((None, kd, HEAD_DIM), sel),
                  pl.BlockSpec((None, kd, HEAD_DIM), sel),
                  pl.BlockSpec((None, HEAD_DIM, HEAD_DIM), sel)],
        out_specs=pl.BlockSpec((None, None, nb, HEAD_DIM), lambda bi, c: (bi, c, 0, 0)),
        out_shape=jax.ShapeDtypeStruct((b, 2 * NSA_KV, nb, HEAD_DIM), BF16),
        compiler_params=_cparams("parallel", "parallel"),
        name="nsa_compress",
    )(x4, pelo, pehi, w1lo, w1hi, w2)


def _stack_heads(q):
    return jnp.concatenate([q[:, g * HEAD_DIM:(g + 1) * HEAD_DIM] for g in range(NSA_GROUP)], axis=0)


def _nsa_cmp_kernel(q_ref, kc_ref, vct_ref, gate_ref, selmap_t_ref, oc_ref, bias_ref, *, n_cmp, n_sel, topk):
    tq = q_ref.shape[0]
    s0 = pl.program_id(2) * tq
    q4 = _stack_heads(q_ref[...])
    st = _dot_nt(kc_ref[...], q4)
    t1 = s0 + lax.broadcasted_iota(jnp.int32, (1, tq), 1)
    t = jnp.concatenate([t1] * NSA_GROUP, axis=1)
    n = lax.broadcasted_iota(jnp.int32, (st.shape[0], 1), 0)
    valid = (n * CMP_STRIDE + (CMP_LEN - 1) <= t) & (n < n_cmp)
    sm = jnp.where(valid, st, NEG_INF)
    e = jnp.where(valid, jnp.exp2(sm - jnp.max(sm, axis=0, keepdims=True)), 0.0)
    l = jnp.sum(e, axis=0, keepdims=True)
    p = e * (1.0 / jnp.where(l > 0.0, l, 1.0))
    o_t = _dot(vct_ref[...], p.astype(BF16))

    gates_t = jax.nn.sigmoid(gate_ref[...]).T
    for g in range(NSA_GROUP):
        r = g * N_BRANCH
        oc_ref[:, g * HEAD_DIM:(g + 1) * HEAD_DIM] = (gates_t[r:r + 1] * o_t[:, g * tq:(g + 1) * tq]).T

    ps = p[:, 0:tq]
    for g in range(1, NSA_GROUP):
        ps = ps + p[:, g * tq:(g + 1) * tq]
    ps_hi = ps.astype(BF16)
    ps_lo = (ps - ps_hi.astype(F32)).astype(BF16)
    imp = _dot(selmap_t_ref[...], ps_hi) + _dot(selmap_t_ref[...], ps_lo)
    blk = lax.broadcasted_iota(jnp.int32, (LANES, 1), 0)
    cur = t1 // SEL_LEN
    forced = (blk == 0) | (blk == cur) | (blk == cur - 1)
    imp = jnp.where(blk > cur, -BIG, jnp.where(forced, BIG, imp))
    imp = jnp.where(blk < n_sel, imp, -jnp.inf)
    sel = jnp.zeros(imp.shape, F32)
    blk_f = blk.astype(F32)
    for _ in range(topk):
        mx = jnp.max(imp, axis=0, keepdims=True)
        idx = jnp.min(jnp.where(imp == mx, blk_f, float(LANES)), axis=0, keepdims=True)
        pick = blk_f == idx
        sel = jnp.where(pick, 1.0, sel)
        imp = jnp.where(pick, -jnp.inf, imp)
    bias_ref[...] = jnp.where(sel > 0.5, 0.0, NEG_INF).T.astype(bias_ref.dtype)


def nsa_cmp_select(proj, kvc, vct, gates, selmap_t, batch, seq, tq=256):
    nb = kvc.shape[2]
    tq = min(tq, seq)
    nq = seq // tq
    n_sel = seq // SEL_LEN
    kern = functools.partial(_nsa_cmp_kernel, n_cmp=nb - 1, n_sel=n_sel, topk=min(SEL_TOPK, n_sel))
    gw = NSA_GROUP * HEAD_DIM
    return pl.pallas_call(
        kern,
        grid=(batch, NSA_KV, nq),
        in_specs=[pl.BlockSpec((tq, gw), lambda b, h, i: (b * nq + i, h)),
                  pl.BlockSpec((None, None, nb, HEAD_DIM), lambda b, h, i: (b, h, 0, 0)),
                  pl.BlockSpec((None, None, HEAD_DIM, nb), lambda b, h, i: (b, h, 0, 0)),
                  pl.BlockSpec((tq, LANES), lambda b, h, i: (b * nq + i, h)),
                  pl.BlockSpec((LANES, nb), lambda b, h, i: (0, 0))],
        out_specs=[pl.BlockSpec((tq, gw), lambda b, h, i: (b * nq + i, h)),
                   pl.BlockSpec((None, None, tq, LANES), lambda b, h, i: (b, h, i, 0))],
        out_shape=[jax.ShapeDtypeStruct((batch * seq, NSA_HEADS * HEAD_DIM), F32),
                   jax.ShapeDtypeStruct((batch, NSA_KV, seq, LANES), BF16)],
        compiler_params=_cparams("parallel", "parallel", "arbitrary"),
        name="nsa_cmp_select",
    )(proj, kvc, vct, gates, selmap_t)


def _flash_update_t(st, vt_blk, m_ref, l_ref, acc_ref, c, valid):
    if valid is not None:
        st = jnp.where(valid, st, NEG_INF)
    m_prev = m_ref[c]
    m_new = jnp.maximum(m_prev, jnp.max(st, axis=0, keepdims=True))
    p = jnp.exp2(st - m_new)
    alpha = jnp.exp2(m_prev - m_new)
    l_ref[c] = alpha * l_ref[c] + jnp.sum(p, axis=0, keepdims=True)
    acc_ref[c] = alpha * acc_ref[c] + _dot(vt_blk, p.astype(BF16))
    m_ref[c] = m_new


def _causal_chains(diag, nj, cw, tk):
    if diag is None:
        return [(j, False) for j in range(nj)]
    lo, hi = diag * tk, (diag + 1) * tk - 1
    return [(j, hi > j * cw) for j in range(nj) if lo <= (j + 1) * cw - 1]


def _nsa_sel_win_kernel(q_ref, ks_ref, vst_ref, kw_ref, vwt_ref, bias_ref, expand_ref, gate_ref, oc_ref,
                        o_ref, m_ref, l_ref, acc_ref, *, tk):
    tq = q_ref.shape[0]
    cw = acc_ref.shape[2]
    nj = tq // cw
    s0 = pl.program_id(2) * tq
    q = q_ref[...]
    qc = [[q[j * cw:(j + 1) * cw, g * HEAD_DIM:(g + 1) * HEAD_DIM] for j in range(nj)] for g in range(NSA_GROUP)]
    t = s0 + lax.broadcasted_iota(jnp.int32, (1, tq), 1)
    bias = bias_ref[...]
    qx = [[jnp.concatenate([qc[g][j], bias[j * cw:(j + 1) * cw]], axis=1) for j in range(nj)]
          for g in range(NSA_GROUP)]

    m_ref[...] = jnp.full(m_ref.shape, NEG_INF, F32)
    l_ref[...] = jnp.zeros(l_ref.shape, F32)
    acc_ref[...] = jnp.zeros(acc_ref.shape, F32)

    def step(kb, diag):
        k0 = pl.multiple_of(kb * tk, tk)
        kx = jnp.concatenate([ks_ref[pl.ds(k0, tk), :], expand_ref[pl.ds(k0, tk), :]], axis=1)
        vt_blk = vst_ref[kb]
        todo = _causal_chains(diag, nj, cw, tk)
        scores = {(g, j): _dot_nt(kx, qx[g][j]) for j, _ in todo for g in range(NSA_GROUP)}
        for j, masked in todo:
            valid = None
            if masked:
                valid = k0 + lax.broadcasted_iota(jnp.int32, (tk, 1), 0) <= t[:, j * cw:(j + 1) * cw]
            for g in range(NSA_GROUP):
                _flash_update_t(scores[(g, j)], vt_blk, m_ref, l_ref, acc_ref, g * nj + j, valid)

    n_full = s0 // tk

    def full_body(kb, c):
        step(kb, None)
        return c

    lax.fori_loop(0, n_full, full_body, 0)
    for d in range(tq // tk):
        step(n_full + d, d)

    nwb = WIN // cw + 1
    gates_t = jax.nn.sigmoid(gate_ref[...]).T
    for j in range(nj):
        j0 = jnp.maximum(s0 // cw + j - WIN // cw, 0)
        w0 = pl.multiple_of(j0 * cw, cw)
        kwblk = kw_ref[pl.ds(w0, nwb * cw), :]
        tj = t[:, j * cw:(j + 1) * cw]
        dlt = tj - (w0 + lax.broadcasted_iota(jnp.int32, (nwb * cw, 1), 0))
        inwin = (dlt >= 0) & (dlt < WIN)
        wscores = [_dot_nt(kwblk, qc[g][j]) for g in range(NSA_GROUP)]
        for g in range(NSA_GROUP):
            sw = jnp.where(inwin, wscores[g], NEG_INF)
            pw = jnp.exp2(sw - jnp.max(sw, axis=0, keepdims=True))
            lw = jnp.sum(pw, axis=0, keepdims=True)
            pw = pw.astype(BF16)
            o_win = _dot(vwt_ref[j0], pw[0:cw])
            for jj in range(1, nwb):
                o_win = o_win + _dot(vwt_ref[j0 + jj], pw[jj * cw:(jj + 1) * cw])
            r = g * N_BRANCH
            c = g * nj + j
            g_sel = gates_t[r + 1:r + 2, j * cw:(j + 1) * cw]
            g_win = gates_t[r + 2:r + 3, j * cw:(j + 1) * cw]
            o_t = g_sel * (acc_ref[c] / l_ref[c]) + g_win * (o_win / lw)
            rs, cs = slice(j * cw, (j + 1) * cw), slice(g * HEAD_DIM, (g + 1) * HEAD_DIM)
            o_ref[rs, cs] = (oc_ref[rs, cs] + o_t.T).astype(o_ref.dtype)


def nsa_sel_win(proj, vst, vwt, selbias, expand, gates, oc, batch, seq, tq, tk):
    nq = seq // tq
    cw = vwt.shape[-1]
    nch = NSA_GROUP * tq // cw
    gw = NSA_GROUP * HEAD_DIM
    kv_spec = lambda cb: pl.BlockSpec((seq, HEAD_DIM), lambda b, h, i: (b, cb + h))
    vt_spec = lambda blk: pl.BlockSpec((None, None, seq // blk, HEAD_DIM, blk), lambda b, h, i: (b, h, 0, 0, 0))
    return pl.pallas_call(
        functools.partial(_nsa_sel_win_kernel, tk=tk),
        grid=(batch, NSA_KV, nq),
        in_specs=[pl.BlockSpec((tq, gw), lambda b, h, i: (b * nq + i, h)),
                  kv_spec(CB_KSEL), vt_spec(tk), kv_spec(CB_KWIN), vt_spec(cw),
                  pl.BlockSpec((None, None, tq, LANES), lambda b, h, i: (b, h, i, 0)),
                  pl.BlockSpec((seq, LANES), lambda b, h, i: (0, 0)),
                  pl.BlockSpec((tq, LANES), lambda b, h, i: (b * nq + i, h)),
                  pl.BlockSpec((tq, gw), lambda b, h, i: (b * nq + i, h))],
        out_specs=pl.BlockSpec((tq, gw), lambda b, h, i: (b * nq + i, h)),
        out_shape=jax.ShapeDtypeStruct((batch * seq, NSA_HEADS * HEAD_DIM), BF16),
        scratch_shapes=[pltpu.VMEM((nch, 1, cw), F32), pltpu.VMEM((nch, 1, cw), F32),
                        pltpu.VMEM((nch, HEAD_DIM, cw), F32)],
        compiler_params=_cparams("parallel", "parallel", "arbitrary"),
        name="nsa_sel_win",
    )(proj, proj, vst, proj, vwt, selbias, expand, gates, oc)


def _diff_attn_kernel(q_ref, k_ref, vt_ref, lamp_ref, subln_ref, o_ref, m_ref, l_ref, acc_ref, *, tk, lam_init):
    tq = q_ref.shape[0]
    s0 = pl.program_id(2) * tq
    q = q_ref[...]
    lane = lax.broadcasted_iota(jnp.int32, (1, LANES), 1)
    zero = jnp.zeros_like(q)
    qmaps = (jnp.where(lane < DIFF_QK, q, zero), jnp.where(lane >= DIFF_QK, q, zero))
    cw = acc_ref.shape[2]
    nj = tq // cw
    t = s0 + lax.broadcasted_iota(jnp.int32, (1, tq), 1)

    m_ref[...] = jnp.full(m_ref.shape, NEG_INF, F32)
    l_ref[...] = jnp.zeros(l_ref.shape, F32)
    acc_ref[...] = jnp.zeros(acc_ref.shape, F32)

    def step(kb, diag):
        k0 = pl.multiple_of(kb * tk, tk)
        kblk = k_ref[pl.ds(k0, tk), :]
        vt_blk = vt_ref[kb]
        todo = _causal_chains(diag, nj, cw, tk)
        chains = [(c, j, masked) for j, masked in todo for c in range(2)]
        scores = [_dot_nt(kblk, qmaps[c][j * cw:(j + 1) * cw]) for c, j, _ in chains]
        for (c, j, masked), st in zip(chains, scores):
            valid = None
            if masked:
                valid = k0 + lax.broadcasted_iota(jnp.int32, (tk, 1), 0) <= t[:, j * cw:(j + 1) * cw]
            _flash_update_t(st, vt_blk, m_ref, l_ref, acc_ref, c * nj + j, valid)

    n_full = s0 // tk

    def full_body(kb, c):
        step(kb, None)
        return c

    lax.fori_loop(0, n_full, full_body, 0)
    for d in range(tq // tk):
        step(n_full + d, d)

    lp = lamp_ref[...]
    lam = (jnp.exp(jnp.sum(lp[0:1] * lp[1:2], axis=-1, keepdims=True))
           - jnp.exp(jnp.sum(lp[2:3] * lp[3:4], axis=-1, keepdims=True)) + lam_init)
    for j in range(nj):
        a = acc_ref[j] / l_ref[j] - lam * (acc_ref[nj + j] / l_ref[nj + j])
        y = a * lax.rsqrt(jnp.mean(a * a, axis=0, keepdims=True) + EPS) * subln_ref[...]
        o_ref[j * cw:(j + 1) * cw, :] = (y * (1.0 - lam_init)).T.astype(o_ref.dtype)


def diff_attention(proj, vt, lam_params, subln, lam_init, batch, seq, tq, tk):
    nq = seq // tq
    cw = min(CHAIN_W, tq)
    return pl.pallas_call(
        functools.partial(_diff_attn_kernel, tk=tk, lam_init=lam_init),
        grid=(batch, DIFF_HEADS, nq),
        in_specs=[pl.BlockSpec((tq, HEAD_DIM), lambda b, h, i: (b * nq + i, CB_QB + h)),
                  pl.BlockSpec((seq, HEAD_DIM), lambda b, h, i: (b, CB_KB + h)),
                  pl.BlockSpec((None, None, seq // tk, HEAD_DIM, tk), lambda b, h, i: (b, h, 0, 0, 0)),
                  pl.BlockSpec((4, DIFF_QK), lambda b, h, i: (0, 0)),
                  pl.BlockSpec((None, HEAD_DIM, 1), lambda b, h, i: (h, 0, 0))],
        out_specs=pl.BlockSpec((tq, HEAD_DIM), lambda b, h, i: (b * nq + i, h)),
        out_shape=jax.ShapeDtypeStruct((batch * seq, DIFF_HEADS * HEAD_DIM), BF16),
        scratch_shapes=[pltpu.VMEM((2 * tq // cw, 1, cw), F32), pltpu.VMEM((2 * tq // cw, 1, cw), F32),
                        pltpu.VMEM((2 * tq // cw, HEAD_DIM, cw), F32)],
        compiler_params=_cparams("parallel", "parallel", "arbitrary"),
        name="diff_attention",
    )(proj, proj, vt, lam_params, subln)


def _lru_kernel(xb_ref, yb_ref, cw_ref, cb_ref, wa_ref, wx_ref, ba_ref, bx_ref, lam_ref, o_ref,
                xpad_ref, a_ref, u_ref, h_ref, carry_ref):
    tt = xb_ref.shape[0]
    pad = SUBLANES

    @pl.when(pl.program_id(2) == 0)
    def _():
        xpad_ref[0:pad, :] = jnp.zeros((pad, xpad_ref.shape[1]), F32)
        carry_ref[...] = jnp.zeros_like(carry_ref)

    xb = xb_ref[...]
    xpad_ref[pad:pad + tt, :] = xb
    cw = cw_ref[...]
    xc = cb_ref[...] + cw[CONV_W - 1:CONV_W] * xb
    for w in range(CONV_W - 1):
        off = pad - (CONV_W - 1) + w
        xc = xc + cw[w:w + 1] * xpad_ref[off:off + tt, :]
    xpad_ref[0:pad, :] = xb[tt - pad:tt]

    xcb = xc.astype(BF16)
    r = jax.nn.sigmoid(_dot(xcb, wa_ref[...]) + ba_ref[...])
    i = jax.nn.sigmoid(_dot(xcb, wx_ref[...]) + bx_ref[...])
    nl = -lam_ref[...]
    softplus = jnp.maximum(nl, 0.0) + jnp.log1p(jnp.exp(-jnp.abs(nl)))
    log_a = -LRU_C * r * softplus
    a = jnp.exp(log_a)
    a_ref[...] = a
    u_ref[...] = jnp.sqrt(1.0 - a * a) * (i * xc)

    row = lax.broadcasted_iota(jnp.int32, (SUBLANES, a.shape[1]), 0)

    def slab(j, hprev):
        r0 = pl.multiple_of(j * SUBLANES, SUBLANES)
        aa = a_ref[pl.ds(r0, SUBLANES), :]
        bb = u_ref[pl.ds(r0, SUBLANES), :]
        for d in (1, 2, 4):
            a_sh = jnp.where(row >= d, pltpu.roll(aa, shift=d, axis=0), 1.0)
            b_sh = jnp.where(row >= d, pltpu.roll(bb, shift=d, axis=0), 0.0)
            bb = aa * b_sh + bb
            aa = aa * a_sh
        hh = bb + aa * hprev
        h_ref[pl.ds(r0, SUBLANES), :] = hh
        return hh[SUBLANES - 1:SUBLANES, :]

    carry_ref[...] = lax.fori_loop(0, tt // SUBLANES, slab, carry_ref[...], unroll=4)
    o_ref[...] = (h_ref[...] * jax.nn.gelu(yb_ref[...])).astype(o_ref.dtype)


def lru(proj, conv_w, conv_b, wa, wx, ba, bx, lam, batch, seq, tt=512):
    w = proj.shape[1] // 2
    bw = w // LRU_BLOCKS
    nt = seq // tt
    vec = lambda: pl.BlockSpec((1, bw), lambda b, n, t: (0, n))
    return pl.pallas_call(
        _lru_kernel,
        grid=(batch, LRU_BLOCKS, nt),
        in_specs=[pl.BlockSpec((tt, bw), lambda b, n, t: (b * nt + t, n)),
                  pl.BlockSpec((tt, bw), lambda b, n, t: (b * nt + t, LRU_BLOCKS + n)),
                  pl.BlockSpec((CONV_W, bw), lambda b, n, t: (0, n)),
                  vec(),
                  pl.BlockSpec((None, bw, bw), lambda b, n, t: (n, 0, 0)),
                  pl.BlockSpec((None, bw, bw), lambda b, n, t: (n, 0, 0)),
                  vec(), vec(), vec()],
        out_specs=pl.BlockSpec((tt, bw), lambda b, n, t: (b * nt + t, n)),
        out_shape=jax.ShapeDtypeStruct((batch * seq, w), BF16),
        scratch_shapes=[pltpu.VMEM((tt + SUBLANES, bw), F32), pltpu.VMEM((tt, bw), F32),
                        pltpu.VMEM((tt, bw), F32), pltpu.VMEM((tt, bw), F32), pltpu.VMEM((1, bw), F32)],
        compiler_params=_cparams("parallel", "parallel", "arbitrary"),
        name="rglru",
    )(proj, proj, conv_w, conv_b, wa, wx, ba, bx, lam)


def _sel_constants(seq):
    n_blk = seq // CMP_STRIDE
    n_sel = seq // SEL_LEN
    c0 = np.arange(n_blk)[:, None] * CMP_STRIDE
    j0 = np.arange(LANES)[None, :] * SEL_LEN
    ov = np.clip(np.minimum(c0 + CMP_LEN, j0 + SEL_LEN) - np.maximum(c0, j0), 0, None) / CMP_LEN
    ov[n_blk - 1:, :] = 0.0
    ov[:, n_sel:] = 0.0
    expand = (np.arange(seq)[:, None] // SEL_LEN == np.arange(LANES)[None, :]).astype(np.float32)
    return jnp.asarray(ov.T, BF16), jnp.asarray(expand, BF16)


def _transposed_blocks(proj, col_block, n_heads, batch, seq, blk):
    v = proj[:, col_block * LANES:(col_block + n_heads) * LANES].reshape(batch, seq // blk, blk, n_heads, HEAD_DIM)
    return v.transpose(0, 3, 1, 4, 2)


def _attn_layer(h, layer_idx, g_mix, w_in, pe_k, w1_k, w2_k, pe_v, w1_v, w2_v,
                lq1, lk1, lq2, lk2, subln, w_out, batch, seq):
    a_q = NSA_HEADS * HEAD_DIM
    a_kv = NSA_KV * HEAD_DIM
    o2 = a_q + 6 * a_kv
    o3 = o2 + NSA_HEADS * N_BRANCH
    w_main = jnp.concatenate([w_in[:, :o2], w_in[:, o3:]], axis=1).astype(BF16)
    ng = NSA_GROUP * N_BRANCH
    w_gate = jnp.concatenate([jnp.pad(w_in[:, o2 + k * ng:o2 + (k + 1) * ng], ((0, 0), (0, LANES - ng)))
                              for k in range(NSA_KV)], axis=1).astype(BF16)
    scale = np.ones((1, N_MAIN_COLS), np.float32)
    scale[:, :a_q] = HEAD_DIM ** -0.5 * LOG2E
    scale[:, CB_QB * LANES:CB_KB * LANES] = DIFF_QK ** -0.5 * LOG2E
    g = g_mix.reshape(1, -1)
    proj, gates = norm_matmul(h, g, w_main, jnp.asarray(scale), BF16, w_side=w_gate)

    nb = seq // CMP_STRIDE
    x4 = proj[:, CB_KCMP * LANES:CB_KSEL * LANES].reshape(batch, nb, CMP_STRIDE, 2 * NSA_KV, HEAD_DIM)
    x4 = x4.transpose(0, 3, 1, 2, 4).reshape(batch, 2 * NSA_KV, nb, CMP_STRIDE * HEAD_DIM)
    flat = lambda a: a.reshape(1, CMP_STRIDE * HEAD_DIM)
    pelo = jnp.stack([flat(pe_k[:CMP_STRIDE]), flat(pe_v[:CMP_STRIDE])])
    pehi = jnp.stack([flat(pe_k[CMP_STRIDE:]), flat(pe_v[CMP_STRIDE:])])
    wflat = lambda a: a.reshape(CMP_STRIDE * HEAD_DIM, HEAD_DIM)
    w1lo = jnp.stack([wflat(w1_k[:CMP_STRIDE]), wflat(w1_v[:CMP_STRIDE])]).astype(BF16)
    w1hi = jnp.stack([wflat(w1_k[CMP_STRIDE:]), wflat(w1_v[CMP_STRIDE:])]).astype(BF16)
    w2 = jnp.stack([w2_k, w2_v]).astype(BF16)
    kvc = compress(x4, pelo, pehi, w1lo, w1hi, w2)

    selmap_t, expand = _sel_constants(seq)
    vct = kvc[:, NSA_KV:].transpose(0, 1, 3, 2)
    oc, selbias = nsa_cmp_select(proj, kvc, vct, gates, selmap_t, batch, seq)
    tq, tk = min(NSA_TQ, seq), min(NSA_TK, seq)
    vst = _transposed_blocks(proj, CB_VSEL, NSA_KV, batch, seq, tk)
    vwt = _transposed_blocks(proj, CB_VWIN, NSA_KV, batch, seq, min(CHAIN_W, tq))
    o_a = nsa_sel_win(proj, vst, vwt, selbias, expand, gates, oc, batch, seq, tq, tk)

    lam_init = 0.8 - 0.6 * math.exp(-0.3 * layer_idx)
    lam_params = jnp.stack([lq1, lk1, lq2, lk2]).astype(F32)
    tk = min(DIFF_TK, seq)
    vt = _transposed_blocks(proj, CB_VB, DIFF_HEADS, batch, seq, tk)
    o_b = diff_attention(proj, vt, lam_params, subln.reshape(DIFF_HEADS, HEAD_DIM, 1), lam_init, batch, seq,
                         min(DIFF_TQ, seq), tk)

    w_out = w_out.astype(BF16)
    return matmul_residual([o_a, o_b], [w_out[:a_q], w_out[a_q:]], h)


def _rec_layer(h, g_mix, w_in, conv_w, conv_b, wa, ba, wx, bx, lam_p, w_out, batch, seq):
    n = w_in.shape[1]
    proj = norm_matmul(h, g_mix.reshape(1, -1), w_in.astype(BF16), jnp.ones((1, n), F32), F32)
    row = lambda v: v.reshape(1, -1)
    hy = lru(proj, conv_w, row(conv_b), wa.astype(BF16), wx.astype(BF16), row(ba), row(bx), row(lam_p), batch, seq)
    return matmul_residual([hy], [w_out.astype(BF16)], h)


def kernel(x, p, g_mix, g_ffn, g_ple, g_final, w_in_attn, cmp_pe_k, cmp_w1_k, cmp_w2_k, cmp_pe_v, cmp_w1_v, cmp_w2_v, diff_lq1, diff_lk1, diff_lq2, diff_lk2, diff_subln, w_out_attn, w_in_rec, conv_w, conv_b, lru_wa, lru_ba, lru_wx, lru_bx, lru_lambda, w_out_rec, w_ffn_gate, w_ffn_up, w_ffn_down, w_ple_proj, w_ple_gate):
    batch, seq, d = x.shape
    depth = p.shape[0]
    h = x.reshape(batch * seq, d)
    for i in range(depth):
        j = i // 2
        if i % 2 == 0:
            h = _attn_layer(h, i, g_mix[i], w_in_attn[j], cmp_pe_k[j], cmp_w1_k[j], cmp_w2_k[j],
                            cmp_pe_v[j], cmp_w1_v[j], cmp_w2_v[j], diff_lq1[j], diff_lk1[j],
                            diff_lq2[j], diff_lk2[j], diff_subln[j], w_out_attn[j], batch, seq)
        else:
            h = _rec_layer(h, g_mix[i], w_in_rec[j], conv_w[j], conv_b[j], lru_wa[j], lru_ba[j],
                           lru_wx[j], lru_bx[j], lru_lambda[j], w_out_rec[j], batch, seq)
        h = ffn(h, g_ffn[i].reshape(1, -1), w_ffn_gate[i].astype(BF16), w_ffn_up[i].astype(BF16),
                w_ffn_down[i].astype(BF16))
        g_out = g_final.reshape(1, -1) if i == depth - 1 else None
        h = ple(h, g_ple[i].reshape(1, -1), p[i].reshape(batch * seq, -1),
                w_ple_gate[i].astype(BF16), w_ple_proj[i].astype(BF16), g_out)
    return h.reshape(batch, seq, d)
```
